```python
import math
import jax, jax.numpy as jnp
from jax import lax
import numpy as np

D_MODEL = 1024
BATCH = 8
SEQ = 2048
DEPTH = 1

GRID_W = 64
CTX_LEN = 256
D_MIX = D_MODEL
N_DIRS = 2
S5_WIDTH = D_MIX // 2
S5_GROUP = 16
S5_GROUPS = S5_WIDTH // S5_GROUP
S5_STATE = 64
GDN_HEADS = 4
GDN_HEAD_DIM = (D_MIX - S5_WIDTH) // GDN_HEADS
GDN_WIDTH = GDN_HEADS * GDN_HEAD_DIM
GDN_CHUNK = 64
CONV_K = 3
PROJ_SPLITS = [S5_WIDTH, 3 * GDN_WIDTH, GDN_WIDTH, N_DIRS * GDN_HEADS, N_DIRS * GDN_HEADS]
PROJ_DIM = sum(PROJ_SPLITS)
PEER_HEADS = 8
PEER_N_KEYS = 128
PEER_N_EXPERTS = PEER_N_KEYS * PEER_N_KEYS
PEER_QUERY_DIM = 256
PEER_HALF = PEER_QUERY_DIM // 2
PEER_TOPK = 16
PEER_BLOCK = 128
EPS = 1e-6

kernel_name = "hymba_s5_gdn_peer_prefix_dit"


def rms_norm(x, g):
    xf = x.astype(jnp.float32)
    y = xf * lax.rsqrt(jnp.mean(xf * xf, axis=-1, keepdims=True) + EPS)
    return (y * g.astype(jnp.float32)).astype(x.dtype)


def l2_normalize(x):
    xf = x.astype(jnp.float32)
    return xf * lax.rsqrt(jnp.sum(xf * xf, axis=-1, keepdims=True) + EPS)


def adaln(cond, w_mod, b_mod):
    m = jax.nn.silu(cond) @ w_mod + b_mod
    return jnp.split(m[:, None, :], 6, axis=-1)


def modulate(h, shift, scale):
    return h * (1.0 + scale) + shift


def split_proj(p):
    idx = [int(i) for i in np.cumsum(PROJ_SPLITS)[:-1]]
    return jnp.split(p, idx, axis=-1)


def s5_discretize(lam_re, lam_im, log_dt, b_re, b_im):
    f32 = jnp.float32
    dt = jnp.exp(log_dt.astype(f32))[:, None]
    lr, li = lam_re.astype(f32), lam_im.astype(f32)
    mag = jnp.exp(lr * dt)
    lb_re, lb_im = mag * jnp.cos(li * dt), mag * jnp.sin(li * dt)
    den = lr * lr + li * li
    nr = lb_re - 1.0
    cr = (nr * lr + lb_im * li) / den
    ci = (lb_im * lr - nr * li) / den
    br, bi = b_re.astype(f32), b_im.astype(f32)
    bb_re = cr[..., None] * br - ci[..., None] * bi
    bb_im = cr[..., None] * bi + ci[..., None] * br
    return lb_re, lb_im, bb_re, bb_im


def _complex_affine_combine(e1, e2):
    a1r, a1i, b1r, b1i = e1
    a2r, a2i, b2r, b2i = e2
    return (a1r * a2r - a1i * a2i,
            a1r * a2i + a1i * a2r,
            a2r * b1r - a2i * b1i + b2r,
            a2r * b1i + a2i * b1r + b2i)


def s5_scan(u, lb_re, lb_im, bb_re, bb_im, h0_re, h0_im, reverse):
    if reverse:
        u = jnp.flip(u, axis=1)
    bu_re = jnp.einsum('blgc,gpc->blgp', u, bb_re)
    bu_im = jnp.einsum('blgc,gpc->blgp', u, bb_im)
    bu_re = bu_re.at[:, 0].add(lb_re * h0_re - lb_im * h0_im)
    bu_im = bu_im.at[:, 0].add(lb_re * h0_im + lb_im * h0_re)
    shape = (1, u.shape[1]) + lb_re.shape
    _, _, h_re, h_im = lax.associative_scan(
        _complex_affine_combine,
        (jnp.broadcast_to(lb_re, shape), jnp.broadcast_to(lb_im, shape), bu_re, bu_im), axis=1)
    fin_re, fin_im = h_re[:, -1], h_im[:, -1]
    if reverse:
        h_re, h_im = jnp.flip(h_re, axis=1), jnp.flip(h_im, axis=1)
    return h_re, h_im, fin_re, fin_im


def s5_readout(h_re, h_im, c_re, c_im):
    return (jnp.einsum('blgp,gcp->blgc', h_re, c_re.astype(jnp.float32))
            - jnp.einsum('blgp,gcp->blgc', h_im, c_im.astype(jnp.float32)))


def s5_mixer(u_x, u_c, lam_re, lam_im, log_dt, b_re, b_im, c_re, c_im, d_skip, w_glu, with_ctx):
    dtype = u_x.dtype

    def groups(u):
        return u.astype(jnp.float32).reshape(u.shape[0], u.shape[1], S5_GROUPS, S5_GROUP)

    ux, uc = groups(u_x), groups(u_c)
    d = d_skip.astype(jnp.float32).reshape(S5_GROUPS, S5_GROUP)
    zero = jnp.zeros((ux.shape[0], S5_GROUPS, S5_STATE), jnp.float32)
    y_x = ux * d
    y_c = uc * d if with_ctx else None
    for dr in range(N_DIRS):
        reverse = dr == 1
        lb_re, lb_im, bb_re, bb_im = s5_discretize(lam_re[dr], lam_im[dr], log_dt[dr], b_re[dr], b_im[dr])
        hc_re, hc_im, fc_re, fc_im = s5_scan(uc, lb_re, lb_im, bb_re, bb_im, zero, zero, reverse)
        hx_re, hx_im, _, _ = s5_scan(ux, lb_re, lb_im, bb_re, bb_im, fc_re, fc_im, reverse)
        y_x = y_x + s5_readout(hx_re, hx_im, c_re[dr], c_im[dr])
        if with_ctx:
            y_c = y_c + s5_readout(hc_re, hc_im, c_re[dr], c_im[dr])

    def glu(y):
        y = jax.nn.gelu(y.reshape(y.shape[0], y.shape[1], S5_WIDTH), approximate=False).astype(dtype)
        return y * jax.nn.sigmoid(y @ w_glu)

    return glu(y_x), (glu(y_c) if with_ctx else None)


def grid_short_conv(x, w, rows, cols):
    bn, length, ch = x.shape
    y = lax.conv_general_dilated(
        x.reshape(bn, rows, cols, ch), w.reshape(CONV_K, CONV_K, 1, ch).astype(x.dtype),
        window_strides=(1, 1), padding='SAME',
        dimension_numbers=('NHWC', 'HWIO', 'NHWC'), feature_group_count=ch)
    return y.reshape(bn, length, ch)


def gdn_prepare(qkv, a, b, rows, cols, conv_w, a_log, dt_bias):
    bn, length, _ = qkv.shape
    qkv = jax.nn.silu(grid_short_conv(qkv, conv_w, rows, cols))
    q, k, v = jnp.split(qkv, 3, axis=-1)

    def heads(t):
        return t.reshape(bn, length, GDN_HEADS, GDN_HEAD_DIM).transpose(0, 2, 1, 3)

    q = l2_normalize(heads(q)) * (GDN_HEAD_DIM ** -0.5)
    k = l2_normalize(heads(k))
    v = heads(v).astype(jnp.float32)
    a = a.astype(jnp.float32).reshape(bn, length, N_DIRS, GDN_HEADS)
    b = b.astype(jnp.float32).reshape(bn, length, N_DIRS, GDN_HEADS)
    g = -jnp.exp(a_log.astype(jnp.float32)) * jax.nn.softplus(a + dt_bias.astype(jnp.float32))
    beta = jax.nn.sigmoid(b)
    return q, k, v, g.transpose(2, 0, 3, 1), beta.transpose(2, 0, 3, 1)


def gated_delta_chunked(q, k, v, g, beta, s0):
    bn, nh, length, dk = q.shape
    dv = v.shape[-1]
    cs = GDN_CHUNK
    nc = length // cs
    q = q.reshape(bn, nh, nc, cs, dk)
    k = k.reshape(bn, nh, nc, cs, dk)
    v = v.reshape(bn, nh, nc, cs, dv)
    g = jnp.cumsum(g.reshape(bn, nh, nc, cs), axis=-1)
    beta = beta.reshape(bn, nh, nc, cs)
    incl = jnp.tril(jnp.ones((cs, cs), bool))
    strict = jnp.tril(jnp.ones((cs, cs), bool), -1)
    diff = g[..., :, None] - g[..., None, :]
    decay = jnp.where(incl, jnp.exp(jnp.where(incl, diff, 0.0)), 0.0)
    kb = k * beta[..., None]
    lmat = jnp.where(strict, jnp.einsum('bhncd,bhnsd->bhncs', kb, k) * decay, 0.0)
    eye = jnp.eye(cs, dtype=jnp.float32)
    tmat = lax.linalg.triangular_solve(eye + lmat, jnp.broadcast_to(eye, lmat.shape),
                                       left_side=True, lower=True, unit_diagonal=True)
    u = jnp.einsum('bhncs,bhnsd->bhncd', tmat, v * beta[..., None])
    w = jnp.einsum('bhncs,bhnsd->bhncd', tmat, kb * jnp.exp(g)[..., None])
    attn = jnp.where(incl, jnp.einsum('bhncd,bhnsd->bhncs', q, k) * decay, 0.0)
    q_dec = q * jnp.exp(g)[..., None]
    k_dec = k * jnp.exp(g[..., -1:] - g)[..., None]
    g_tot = jnp.exp(g[..., -1])
    xs = tuple(jnp.moveaxis(t, 2, 0) for t in (w, u, q_dec, attn, k_dec, g_tot))

    def step(s, inp):
        w_i, u_i, q_i, a_i, k_i, gt_i = inp
        v_new = u_i - jnp.einsum('bhck,bhkv->bhcv', w_i, s)
        o_i = jnp.einsum('bhck,bhkv->bhcv', q_i, s) + jnp.einsum('bhcs,bhsv->bhcv', a_i, v_new)
        s = s * gt_i[..., None, None] + jnp.einsum('bhck,bhcv->bhkv', k_i, v_new)
        return s, o_i

    s_fin, o = lax.scan(step, s0, xs)
    return jnp.moveaxis(o, 0, 2).reshape(bn, nh, length, dv), s_fin


def gdn_mixer(qkv_x, z_x, a_x, b_x, qkv_c, z_c, a_c, b_c, rows, conv_w, a_log, dt_bias, norm_g, with_ctx):
    qx, kx, vx, gx, bx = gdn_prepare(qkv_x, a_x, b_x, rows, GRID_W, conv_w, a_log, dt_bias)
    qc, kc, vc, gc, bc = gdn_prepare(qkv_c, a_c, b_c, 1, qkv_c.shape[1], conv_w, a_log, dt_bias)
    zero = jnp.zeros((qx.shape[0], GDN_HEADS, GDN_HEAD_DIM, GDN_HEAD_DIM), jnp.float32)
    o_x = jnp.zeros_like(vx)
    o_c = jnp.zeros_like(vc)
    for dr in range(N_DIRS):
        if dr == 1:
            flip = lambda t: jnp.flip(t, axis=2)
        else:
            flip = lambda t: t
        oc, sc = gated_delta_chunked(flip(qc), flip(kc), flip(vc), flip(gc[dr]), flip(bc[dr]), zero)
        ox, _ = gated_delta_chunked(flip(qx), flip(kx), flip(vx), flip(gx[dr]), flip(bx[dr]), sc)
        o_x = o_x + flip(ox)
        o_c = o_c + flip(oc)

    def out(o, z):
        bn, nh, length, dv = o.shape
        o = rms_norm(o.transpose(0, 2, 1, 3), norm_g)
        gate = jax.nn.silu(z.astype(jnp.float32)).reshape(bn, length, nh, dv)
        return (o * gate).reshape(bn, length, GDN_WIDTH).astype(z.dtype)

    return out(o_x, z_x), (out(o_c, z_c) if with_ctx else None)


def peer_ffn(h, w_q, sub_keys, expert_u, expert_v):
    bn, length, d = h.shape
    nt = bn * length
    t = h.reshape(nt, d)
    q = (t @ w_q).astype(jnp.float32).reshape(nt, PEER_HEADS, 2, PEER_HALF)
    s = jnp.einsum('thpk,hpnk->thpn', q, sub_keys.astype(jnp.float32))
    s1, i1 = lax.top_k(s[:, :, 0], PEER_TOPK)
    s2, i2 = lax.top_k(s[:, :, 1], PEER_TOPK)
    n_cand = PEER_TOPK * PEER_TOPK
    cand_s = (s1[..., :, None] + s2[..., None, :]).reshape(nt, PEER_HEADS, n_cand)
    cand_i = (i1[..., :, None] * PEER_N_KEYS + i2[..., None, :]).reshape(nt, PEER_HEADS, n_cand)
    top_s, pos = lax.top_k(cand_s, PEER_TOPK)
    idx = jnp.take_along_axis(cand_i, pos, axis=-1)
    gate = jax.nn.softmax(top_s, axis=-1).astype(h.dtype)
    kk = PEER_HEADS * PEER_TOPK
    nb = nt // PEER_BLOCK

    def block(args):
        tb, ib, gb = args
        act = jax.nn.gelu(jnp.einsum('tkd,td->tk', expert_u[ib], tb), approximate=False) * gb
        return jnp.einsum('tk,tkd->td', act, expert_v[ib])

    out = lax.map(block, (t.reshape(nb, PEER_BLOCK, d), idx.reshape(nb, PEER_BLOCK, kk),
                          gate.reshape(nb, PEER_BLOCK, kk)))
    return out.reshape(bn, length, d)


def setup_inputs(seed: int = 0) -> dict:
    key = jax.random.key(seed)
    ks = list(jax.random.split(key, 32))
    f32 = jnp.float32
    D = D_MODEL

    def nrm(k, shape, scale):
        return jax.random.normal(k, shape, f32) * scale

    def gain(k, shape):
        return 1.0 + 0.02 * jax.random.normal(k, shape, f32)

    n_idx = jnp.arange(S5_STATE, dtype=f32)
    s5_shape = (DEPTH, N_DIRS, S5_GROUPS, S5_STATE)
    dt_g = jnp.exp(jax.random.uniform(ks[20], (DEPTH, N_DIRS, GDN_HEADS), f32,
                                      minval=math.log(1e-3), maxval=math.log(1e-1)))
    return {
        "x": jax.random.normal(ks[0], (BATCH, SEQ, D), f32),
        "c": jax.random.normal(ks[1], (BATCH, D), f32),
        "ctx": jax.random.normal(ks[2], (BATCH, CTX_LEN, D), f32),
        "c_ctx": jax.random.normal(ks[3], (D,), f32),
        "w_mod": nrm(ks[4], (DEPTH, D, 6 * D), 0.5 * D ** -0.5),
        "b_mod": nrm(ks[5], (DEPTH, 6 * D), 0.01),
        "g_pre_mix": gain(ks[6], (DEPTH, D)),
        "g_post_mix": gain(ks[7], (DEPTH, D)),
        "g_pre_ffn": gain(ks[8], (DEPTH, D)),
        "g_post_ffn": gain(ks[9], (DEPTH, D)),
        "w_in": nrm(ks[10], (DEPTH, D, PROJ_DIM), D ** -0.5),
        "w_out": nrm(ks[11], (DEPTH, D_MIX, D), D_MIX ** -0.5),
        "s5_lam_re": -0.5 + nrm(ks[12], s5_shape, 0.01),
        "s5_lam_im": math.pi * n_idx + nrm(ks[13], s5_shape, 0.01),
        "s5_log_dt": jax.random.uniform(ks[14], (DEPTH, N_DIRS, S5_GROUPS), f32,
                                        minval=math.log(1e-3), maxval=math.log(1e-1)),
        "s5_b_re": nrm(ks[15], s5_shape + (S5_GROUP,), (2 * S5_GROUP) ** -0.5),
        "s5_b_im": nrm(ks[16], s5_shape + (S5_GROUP,), (2 * S5_GROUP) ** -0.5),
        "s5_c_re": nrm(ks[17], (DEPTH, N_DIRS, S5_GROUPS, S5_GROUP, S5_STATE), (2 * S5_STATE) ** -0.5),
        "s5_c_im": nrm(ks[18], (DEPTH, N_DIRS, S5_GROUPS, S5_GROUP, S5_STATE), (2 * S5_STATE) ** -0.5),
        "s5_d": nrm(ks[19], (DEPTH, S5_WIDTH), 1.0),
        "s5_w_glu": nrm(ks[21], (DEPTH, S5_WIDTH, S5_WIDTH), S5_WIDTH ** -0.5),
        "gdn_conv": nrm(ks[22], (DEPTH, CONV_K, CONV_K, 3 * GDN_WIDTH), 1.0 / CONV_K),
        "gdn_a_log": jnp.log(jax.random.uniform(ks[23], (DEPTH, N_DIRS, GDN_HEADS), f32, minval=1.0, maxval=16.0)),
        "gdn_dt_bias": dt_g + jnp.log(-jnp.expm1(-dt_g)),
        "gdn_norm": gain(ks[24], (DEPTH, GDN_HEAD_DIM)),
        "peer_w_q": nrm(ks[25], (DEPTH, D, PEER_HEADS * PEER_QUERY_DIM), D ** -0.5),
        "peer_keys": nrm(ks[26], (DEPTH, PEER_HEADS, 2, PEER_N_KEYS, PEER_HALF), PEER_HALF ** -0.5),
        "peer_u": nrm(ks[27], (DEPTH, PEER_N_EXPERTS, D), D ** -0.5),
        "peer_v": nrm(ks[28], (DEPTH, PEER_N_EXPERTS, D), PEER_HEADS ** -0.5),
    }


def reference(x, c, ctx, c_ctx, w_mod, b_mod, g_pre_mix, g_post_mix, g_pre_ffn, g_post_ffn,
              w_in, w_out, s5_lam_re, s5_lam_im, s5_log_dt, s5_b_re, s5_b_im, s5_c_re, s5_c_im,
              s5_d, s5_w_glu, gdn_conv, gdn_a_log, gdn_dt_bias, gdn_norm,
              peer_w_q, peer_keys, peer_u, peer_v):
    rows = x.shape[1] // GRID_W
    xc = ctx
    for layer in range(DEPTH):
        with_ctx = layer < DEPTH - 1
        mx = adaln(c, w_mod[layer], b_mod[layer])
        mc = adaln(c_ctx[None], w_mod[layer], b_mod[layer])

        hx = modulate(rms_norm(x, g_pre_mix[layer]), mx[0], mx[1])
        hc = modulate(rms_norm(xc, g_pre_mix[layer]), mc[0], mc[1])
        ux, qkvx, zx, ax, bx = split_proj(hx @ w_in[layer])
        uc, qkvc, zc, ac, bc = split_proj(hc @ w_in[layer])
        s5x, s5c = s5_mixer(ux, uc, s5_lam_re[layer], s5_lam_im[layer], s5_log_dt[layer],
                            s5_b_re[layer], s5_b_im[layer], s5_c_re[layer], s5_c_im[layer],
                            s5_d[layer], s5_w_glu[layer], with_ctx)
        gdx, gdc = gdn_mixer(qkvx, zx, ax, bx, qkvc, zc, ac, bc, rows, gdn_conv[layer],
                             gdn_a_log[layer], gdn_dt_bias[layer], gdn_norm[layer], with_ctx)
        x = x + mx[2] * rms_norm(jnp.concatenate([s5x, gdx], axis=-1) @ w_out[layer], g_post_mix[layer])
        if with_ctx:
            xc = xc + mc[2] * rms_norm(jnp.concatenate([s5c, gdc], axis=-1) @ w_out[layer], g_post_mix[layer])

        hx = modulate(rms_norm(x, g_pre_ffn[layer]), mx[3], mx[4])
        x = x + mx[5] * rms_norm(peer_ffn(hx, peer_w_q[layer], peer_keys[layer], peer_u[layer], peer_v[layer]),
                                 g_post_ffn[layer])
        if with_ctx:
            hc = modulate(rms_norm(xc, g_pre_ffn[layer]), mc[3], mc[4])
            xc = xc + mc[5] * rms_norm(peer_ffn(hc, peer_w_q[layer], peer_keys[layer], peer_u[layer], peer_v[layer]),
                                       g_post_ffn[layer])
    return x
```

```python
import functools

import jax
import jax.numpy as jnp
from jax import lax
from jax.experimental import pallas as pl
from jax.experimental.pallas import tpu as pltpu

F32 = jnp.float32
BF16 = jnp.bfloat16
HI = lax.Precision.HIGHEST

EPS = 1e-6
GRID_W = 64
GDN_CHUNK = 64
S5_CHUNK = 8
PEER_TOPK = 16
LANES = 128
BF16_ROWS = 16
NEG = -1e30
VMEM_LIMIT = 56 * 1024 * 1024


def _cparams(sem):
    return pltpu.CompilerParams(dimension_semantics=sem, vmem_limit_bytes=VMEM_LIMIT)


def _sigmoid(x):
    return 1.0 / (1.0 + jnp.exp(-x))


def _softplus(x):
    return jnp.maximum(x, 0.0) + jnp.log(1.0 + jnp.exp(-jnp.abs(x)))


def _gelu(x):
    return 0.5 * x * (1.0 + lax.erf(x * (2.0 ** -0.5)))


def _rms(x, g):
    return x * lax.rsqrt(jnp.mean(x * x, axis=-1, keepdims=True) + EPS) * g


def _dot(a, b, dims=None, precision=None):
    if dims is None:
        dims = (((a.ndim - 1,), (0,)), ((), ()))
    return lax.dot_general(a, b, dims, precision=precision, preferred_element_type=F32)


def _pack_rows(x):
    w = lax.bitcast_convert_type(x, jnp.uint16).astype(jnp.uint32).reshape(x.shape[0] // 2, 2, x.shape[1])
    return w[:, 0] | (w[:, 1] << 16)


_NT = (((1,), (1,)), ((), ()))
_TN = (((0,), (0,)), ((), ()))


def _adaln_kernel(c_ref, w_ref, b_ref, o_ref):
    c = c_ref[...]
    o_ref[...] = _dot(c * _sigmoid(c), w_ref[...], precision=HI) + b_ref[...]


def _adaln(cond, w_mod, b_mod):
    n, d = cond.shape
    m = w_mod.shape[1]
    tn = 512
    return pl.pallas_call(
        _adaln_kernel,
        grid=(m // tn,),
        in_specs=[pl.BlockSpec((n, d), lambda j: (0, 0)),
                  pl.BlockSpec((d, tn), lambda j: (0, j)),
                  pl.BlockSpec((1, tn), lambda j: (0, j))],
        out_specs=pl.BlockSpec((n, tn), lambda j: (0, j)),
        out_shape=jax.ShapeDtypeStruct((n, m), F32),
        compiler_params=_cparams(("arbitrary",)),
        name="adaln",
    )(cond, w_mod, b_mod.reshape(1, m))


def _premix_kernel(x_ref, mod_ref, g_ref, w_ref, u_ref, qkv_ref, z_ref, ab_ref, *, n_u, n_qkv, n_z):
    m = mod_ref[0]
    h = _rms(x_ref[0], g_ref[...]) * (1.0 + m[1:2]) + m[0:1]
    p = _dot(h.astype(BF16), w_ref[...])
    for j in range(u_ref.shape[0]):
        u_ref[j] = p[:, j * LANES:(j + 1) * LANES].reshape(u_ref.shape[1:])
    qkv_ref[0] = p[:, n_u:n_u + n_qkv]
    z_ref[0] = p[:, n_u + n_qkv:n_u + n_qkv + n_z].astype(BF16)
    ab_ref[0] = p[:, n_u + n_qkv + n_z:]


def _premix(x, mod, per_batch, g, w_pad, n_u, n_qkv, n_z):
    b, l, d = x.shape
    tm = min(512, l)
    tc = S5_CHUNK
    n_all = w_pad.shape[1]
    n_ab = n_all - n_u - n_qkv - n_z
    mod_map = (lambda i, t: (i, 0, 0)) if per_batch else (lambda i, t: (0, 0, 0))
    tok = lambda i, t: (i, t, 0)
    return pl.pallas_call(
        functools.partial(_premix_kernel, n_u=n_u, n_qkv=n_qkv, n_z=n_z),
        grid=(b, l // tm),
        in_specs=[pl.BlockSpec((1, tm, d), tok),
                  pl.BlockSpec((1,) + mod.shape[1:], mod_map),
                  pl.BlockSpec((1, d), lambda i, t: (0, 0)),
                  pl.BlockSpec((d, n_all), lambda i, t: (0, 0))],
        out_specs=[pl.BlockSpec((n_u // LANES, tm // tc, None, tc, LANES), lambda i, t: (0, t, i, 0, 0)),
                   pl.BlockSpec((1, tm, n_qkv), tok),
                   pl.BlockSpec((1, tm, n_z), tok), pl.BlockSpec((1, tm, n_ab), tok)],
        out_shape=[jax.ShapeDtypeStruct((n_u // LANES, l // tc, b, tc, LANES), F32), jax.ShapeDtypeStruct((b, l, n_qkv), F32),
                   jax.ShapeDtypeStruct((b, l, n_z), BF16), jax.ShapeDtypeStruct((b, l, n_ab), F32)],
        compiler_params=_cparams(("parallel", "parallel")),
        name="premix",
    )(x, mod, g, w_pad)


def _s5_matrices(lam_re, lam_im, log_dt, b_re, b_im, c_re, c_im, d_skip):
    t = S5_CHUNK
    n_dirs, g, p = lam_re.shape
    cg = b_re.shape[-1]
    dt = jnp.exp(log_dt)[..., None]
    lr, li = lam_re, lam_im
    den = lr * lr + li * li
    mag = jnp.exp(lr * dt)
    lb_re, lb_im = mag * jnp.cos(li * dt), mag * jnp.sin(li * dt)
    nr = lb_re - 1.0
    cr = (nr * lr + lb_im * li) / den
    ci = (lb_im * lr - nr * li) / den
    bb_re = cr[..., None] * b_re - ci[..., None] * b_im
    bb_im = cr[..., None] * b_im + ci[..., None] * b_re
    taus = jnp.arange(t + 1, dtype=F32)[:, None, None, None]
    pmag = jnp.exp(taus * (lr * dt))
    pw_re = pmag * jnp.cos(taus * (li * dt))
    pw_im = pmag * jnp.sin(taus * (li * dt))
    ein = functools.partial(jnp.einsum, precision=HI)
    lbr = pw_re[..., None] * bb_re - pw_im[..., None] * bb_im
    lbi = pw_re[..., None] * bb_im + pw_im[..., None] * bb_re
    kk = ein('dgcp,tdgpe->tdgce', c_re, lbr) - ein('dgcp,tdgpe->tdgce', c_im, lbi)
    er = c_re[None] * pw_re[:, :, :, None, :] - c_im[None] * pw_im[:, :, :, None, :]
    ei = c_re[None] * pw_im[:, :, :, None, :] + c_im[None] * pw_re[:, :, :, None, :]

    s_idx = jnp.arange(t)
    lag = s_idx[None, :] - s_idx[:, None]
    kf = kk[:, 0][jnp.clip(lag, 0, t)]
    kb = kk[:, 1][jnp.clip(-lag, 0, t)]
    m_full = (jnp.where((lag >= 0)[:, :, None, None, None], kf, 0.0)
              + jnp.where((lag <= 0)[:, :, None, None, None], kb, 0.0))
    eye_t = jnp.eye(t, dtype=F32)[:, :, None, None, None]
    eye_c = jnp.eye(cg, dtype=F32)[None, None, None]
    m_full = m_full + eye_t * eye_c * d_skip.reshape(g, cg)[None, None, :, :, None]
    p_parts = (lbr[:t, 0][::-1], lbi[:t, 0][::-1], lbr[:t, 1], lbi[:t, 1])
    r_parts = (er[1:, 0], -ei[1:, 0], er[1:, 1][::-1], -ei[1:, 1][::-1])

    gw = LANES // cg
    n8 = g // gw
    eye = jnp.eye(gw, dtype=F32)
    m6 = m_full.reshape(t, t, n8, gw, cg, cg).transpose(2, 0, 3, 5, 1, 4)
    mm = (m6[:, :, :, :, :, None, :] * eye[None, None, :, None, None, :, None]).reshape(n8, t * LANES, t * LANES)

    def p_full(a):
        a6 = a.reshape(t, n8, gw, p, cg).transpose(1, 0, 2, 4, 3)
        return a6[:, :, :, :, None, :] * eye[None, None, :, None, :, None]

    pq = jnp.stack([p_full(x) for x in p_parts], axis=4).reshape(n8, t * LANES, 4 * gw * p)

    def r_full(a):
        a6 = a.reshape(t, n8, gw, cg, p).transpose(1, 2, 4, 0, 3)
        return a6[:, :, :, :, None, :] * eye[None, :, None, None, :, None]

    rr = jnp.stack([r_full(x) for x in r_parts], axis=1).reshape(n8, 4 * gw * p, t * LANES)
    dec = jnp.stack([x.reshape(n8, gw * p) for x in (pw_re[t, 0], pw_im[t, 0], pw_re[t, 1], pw_im[t, 1])], axis=1)
    dec = jnp.concatenate([dec, jnp.zeros_like(dec)], axis=1)
    return pq.astype(BF16), mm.astype(BF16), rr.astype(BF16), dec


def _chunk_rows(ref, nb, t):
    n = ref.shape[0]
    return jnp.concatenate([ref[:, :, s, :].reshape(n * nb, LANES) for s in range(t)], axis=1).astype(BF16)


def _s5_state_kernel(x_ref, c_ref, pq_ref, dec_ref, sin_ref, loc_ref, locc_ref, *, nb, t, nct, ncx, ncc, sw):
    rt = pl.program_id(1)
    rows = nct * nb
    loc_ref[pl.ds(pl.multiple_of(rt * rows, rows), rows), :] = _dot(_chunk_rows(x_ref, nb, t), pq_ref[0])

    @pl.when(rt == 0)
    def _():
        locc_ref[...] = _dot(_chunk_rows(c_ref, nb, t), pq_ref[0])

    @pl.when(rt == pl.num_programs(1) - 1)
    def _():
        dec = dec_ref[0]
        far, fai, bar, bai = (jnp.broadcast_to(dec[i:i + 1], (nb, sw)) for i in range(4))

        def step(state, ar, ai, ref, k, c0):
            re, im = state
            r = pl.ds(pl.multiple_of(k * nb, nb), nb)
            return ar * re - ai * im + ref[r, c0:c0 + sw], ar * im + ai * re + ref[r, c0 + sw:c0 + 2 * sw]

        zero = jnp.zeros((nb, sw), F32)

        def ctx_body(i, carry):
            f, bk = carry
            return step(f, far, fai, locc_ref, i, 0), step(bk, bar, bai, locc_ref, ncc - 1 - i, 2 * sw)

        carry = lax.fori_loop(0, ncc, ctx_body, ((zero, zero), (zero, zero)))

        def pair_store(k0, c0, s_lo, s_hi):
            r = pl.ds(pl.multiple_of(k0 * nb, 2 * nb), 2 * nb)
            sin_ref[0, r, c0:c0 + sw] = jnp.concatenate([s_lo[0], s_hi[0]], axis=0).astype(BF16)
            sin_ref[0, r, c0 + sw:c0 + 2 * sw] = jnp.concatenate([s_lo[1], s_hi[1]], axis=0).astype(BF16)

        def x_body(j, carry):
            f0, b0 = carry
            f1 = step(f0, far, fai, loc_ref, 2 * j, 0)
            pair_store(2 * j, 0, f0, f1)
            kb = ncx - 2 - 2 * j
            b1 = step(b0, bar, bai, loc_ref, kb + 1, 2 * sw)
            pair_store(kb, 2 * sw, b1, b0)
            return step(f1, far, fai, loc_ref, 2 * j + 1, 0), step(b1, bar, bai, loc_ref, kb, 2 * sw)

        lax.fori_loop(0, ncx // 2, x_body, carry)


def _s5_out_kernel(x_ref, sin_ref, mm_ref, rr_ref, y_ref, *, nb, t):
    n = x_ref.shape[0]
    y = _dot(_chunk_rows(x_ref, nb, t), mm_ref[0]) + _dot(sin_ref[0], rr_ref[0])
    for s in range(t):
        y_ref[:, :, s, :] = y[:, s * LANES:(s + 1) * LANES].reshape(n, nb, LANES)


def _s5(ux, uc, mats):
    pq, mm, rr, dec = mats
    n8, ncx, nb, t, _ = ux.shape
    ncc = uc.shape[1]
    sw = dec.shape[-1]
    assert nb % 8 == 0 and t == S5_CHUNK and ncx % 2 == 0 and n8 == pq.shape[0]
    nct = min(64, ncx)
    ux5, uc5 = ux, uc
    kdim = t * LANES
    xblk = pl.BlockSpec((None, nct, nb, t, LANES), lambda g, r: (g, r, 0, 0, 0))
    wmap = lambda g, r: (g, 0, 0)
    sin = pl.pallas_call(
        functools.partial(_s5_state_kernel, nb=nb, t=t, nct=nct, ncx=ncx, ncc=ncc, sw=sw),
        grid=(n8, ncx // nct),
        in_specs=[xblk, pl.BlockSpec((None, ncc, nb, t, LANES), lambda g, r: (g, 0, 0, 0, 0)),
                  pl.BlockSpec((1, kdim, 4 * sw), wmap), pl.BlockSpec((1, 8, sw), wmap)],
        out_specs=pl.BlockSpec((1, ncx * nb, 4 * sw), wmap),
        out_shape=jax.ShapeDtypeStruct((n8, ncx * nb, 4 * sw), BF16),
        scratch_shapes=[pltpu.VMEM((ncx * nb, 4 * sw), F32), pltpu.VMEM((ncc * nb, 4 * sw), F32)],
        compiler_params=_cparams(("parallel", "arbitrary")),
        name="s5_state",
    )(ux5, uc5, pq, dec)
    y = pl.pallas_call(
        functools.partial(_s5_out_kernel, nb=nb, t=t),
        grid=(n8, ncx // nct),
        in_specs=[xblk, pl.BlockSpec((1, nct * nb, 4 * sw), lambda g, r: (g, r, 0)),
                  pl.BlockSpec((1, kdim, kdim), wmap), pl.BlockSpec((1, 4 * sw, kdim), wmap)],
        out_specs=xblk,
        out_shape=jax.ShapeDtypeStruct((n8, ncx, nb, t, LANES), F32),
        compiler_params=_cparams(("parallel", "parallel")),
        name="s5_out",
    )(ux5, sin, mm, rr)
    return y


_CONV_PAD = 72


def _gdn_conv_kernel(x_ref, w_ref, o_ref, pad_ref, *, length, cols, two_d, n_norm, n_q, q_scale):
    cb = pl.program_id(1)
    pad = _CONV_PAD
    zeros = jnp.zeros((pad, LANES), F32)
    pad_ref[0:pad, :] = zeros
    pad_ref[pad + length:pad + length + pad, :] = zeros
    pad_ref[pad:pad + length, :] = x_ref[0]
    rc = min(256, length)
    for r0 in range(0, length, rc):
        col = lax.rem(lax.broadcasted_iota(jnp.int32, (rc, LANES), 0) + r0, cols)
        acc = jnp.zeros((rc, LANES), F32)
        for dr in ((0, 1, 2) if two_d else (1,)):
            for dc in range(3):
                off = (dr - 1) * cols + (dc - 1)
                term = pad_ref[pad + r0 + off:pad + r0 + off + rc, :] * w_ref[dr * 3 + dc:dr * 3 + dc + 1, :]
                if dc == 0:
                    term = jnp.where(col >= 1, term, 0.0)
                if dc == 2:
                    term = jnp.where(col <= cols - 2, term, 0.0)
                acc = acc + term
        y = acc * _sigmoid(acc)
        nrm = y * lax.rsqrt(jnp.sum(y * y, axis=-1, keepdims=True) + EPS)
        nrm = nrm * jnp.where(cb < n_q, q_scale, 1.0)
        o_ref[0, r0:r0 + rc, :] = jnp.where(cb < n_norm, nrm, y)


def _gdn_conv(qkv, conv_w, cols, two_d, n_heads, head_dim):
    b, l, ch = qkv.shape
    w9 = jnp.concatenate([conv_w.reshape(9, ch), jnp.zeros((7, ch), F32)], axis=0)
    blk = lambda i, j: (i, 0, j)
    return pl.pallas_call(
        functools.partial(_gdn_conv_kernel, length=l, cols=cols, two_d=two_d,
                          n_norm=2 * n_heads, n_q=n_heads, q_scale=head_dim ** -0.5),
        grid=(b, ch // LANES),
        in_specs=[pl.BlockSpec((1, l, LANES), blk), pl.BlockSpec((16, LANES), lambda i, j: (0, j))],
        out_specs=pl.BlockSpec((1, l, LANES), blk),
        out_shape=jax.ShapeDtypeStruct((b, l, ch), F32),
        scratch_shapes=[pltpu.VMEM((l + 2 * _CONV_PAD, LANES), F32)],
        compiler_params=_cparams(("parallel", "parallel")),
        name="gdn_conv",
    )(qkv, w9)


def _gdn_kernel(qx_ref, kx_ref, vx_ref, abx_ref, qc_ref, kc_ref, vc_ref, abc_ref,
                pl_ref, o_ref, gcx_ref, bcx_ref, gcc_ref, bcc_ref,
                u_ref, w_ref, qd_ref, kd_ref, at_ref, gt_ref, *, n_heads, ncx, ncc):
    cs = GDN_CHUNK
    hd = qx_ref.shape[-1]
    head = pl.program_id(1)
    alog_l, dtb_l = pl_ref[0:1, :], pl_ref[1:2, :]
    lane = lax.broadcasted_iota(jnp.int32, (1, LANES), 1)

    def gate_columns(ab_ref, g_out, b_out, n_rows):
        rc = min(256, n_rows)
        for r0 in range(0, n_rows, rc):
            ab = ab_ref[0, r0:r0 + rc, :]
            g_all = -jnp.exp(alog_l) * _softplus(ab + dtb_l)
            b_all = _sigmoid(ab)
            for dr in range(2):
                j = dr * n_heads + head
                g = jnp.sum(jnp.where(lane == j, g_all, 0.0), axis=-1, keepdims=True)
                bt = jnp.sum(jnp.where(lane == j + 2 * n_heads, b_all, 0.0), axis=-1, keepdims=True)
                g_out[dr, r0:r0 + rc, :] = jnp.broadcast_to(g, (rc, hd))
                b_out[dr, r0:r0 + rc, :] = jnp.broadcast_to(bt, (rc, hd))

    gate_columns(abx_ref, gcx_ref, bcx_ref, ncx * cs)
    gate_columns(abc_ref, gcc_ref, bcc_ref, ncc * cs)

    ri = lax.broadcasted_iota(jnp.int32, (cs, cs), 0)
    ci = lax.broadcasted_iota(jnp.int32, (cs, cs), 1)
    eye = jnp.where(ri == ci, 1.0, 0.0)
    row_id = lax.broadcasted_iota(jnp.int32, (cs, hd), 0)

    def cumsum_rows(x, reverse):
        s = 1
        while s < cs:
            if reverse:
                x = x + jnp.where(row_id < cs - s, pltpu.roll(x, cs - s, 0), 0.0)
            else:
                x = x + jnp.where(row_id >= s, pltpu.roll(x, s, 0), 0.0)
            s *= 2
        return x

    def chunk_local(c, refs, dr, base):
        q_ref, k_ref, v_ref, gcol_ref, bcol_ref = refs
        rows = pl.ds(pl.multiple_of(c * cs, cs), cs)
        q, k, v = q_ref[0, rows, :], k_ref[0, rows, :], v_ref[0, rows, :]
        gb, bb = gcol_ref[dr, rows, :], bcol_ref[dr, rows, :]
        incl = (ri >= ci) if dr == 0 else (ri <= ci)
        strict = (ri > ci) if dr == 0 else (ri < ci)
        gcs = cumsum_rows(gb, dr == 1)
        gcs_row = jnp.concatenate([gcs, gcs], axis=0).T[0:1, 0:cs]
        diff = gcs[:, :cs] - gcs_row
        decay = jnp.where(incl, jnp.exp(jnp.where(incl, diff, 0.0)), 0.0)
        kbf = k.astype(BF16)
        kb = k * bb
        lmat = jnp.where(strict, _dot(kb.astype(BF16), kbf, _NT) * decay, 0.0)
        pw = -lmat
        tmat = eye + pw
        for _ in range(cs.bit_length() - 2):
            pw = _dot(pw.astype(BF16), pw.astype(BF16))
            tmat = tmat + _dot(tmat.astype(BF16), pw.astype(BF16))
        g_exp = jnp.exp(gcs)
        rhs = jnp.concatenate([v * bb, kb * g_exp], axis=1).astype(BF16)
        uw = _dot(tmat.astype(BF16), rhs)
        attn = jnp.where(incl, _dot(q.astype(BF16), kbf, _NT) * decay, 0.0)
        g_last = gcs[cs - 1:cs, :] if dr == 0 else gcs[0:1, :]
        srows = pl.ds(pl.multiple_of((base + c) * cs, cs), cs)
        u_ref[dr, srows, :] = uw[:, :hd]
        w_ref[dr, srows, :] = uw[:, hd:].astype(BF16)
        qd_ref[dr, srows, :] = (q * g_exp).astype(BF16)
        kd_ref[dr, srows, :] = (k * jnp.exp(g_last - gcs)).astype(BF16)
        at_ref[dr, srows, :] = attn.astype(BF16)
        gt_ref[dr, pl.ds(pl.multiple_of((base + c) * 8, 8), 8), :] = jnp.broadcast_to(jnp.exp(g_last), (8, hd))

    def chunk_seq(cg, state, dr):
        srows = pl.ds(pl.multiple_of(cg * cs, cs), cs)
        sb = state.astype(BF16)
        v_new = u_ref[dr, srows, :] - _dot(w_ref[dr, srows, :], sb)
        vb = v_new.astype(BF16)
        o = _dot(qd_ref[dr, srows, :], sb) + _dot(at_ref[dr, srows, :], vb)
        gt = gt_ref[dr, pl.ds(pl.multiple_of(cg * 8, 8), 8), :][0:1]
        return o, state * gt + _dot(kd_ref[dr, srows, :], vb, _TN)

    refs_c = (qc_ref, kc_ref, vc_ref, gcc_ref, bcc_ref)
    refs_x = (qx_ref, kx_ref, vx_ref, gcx_ref, bcx_ref)

    def local_pass(n, refs, base):
        unroll = 2 if n % 2 == 0 else 1

        def body(i, _):
            for k in range(unroll):
                for dr in range(2):
                    chunk_local(i * unroll + k, refs, dr, base)
            return 0

        lax.fori_loop(0, n // unroll, body, 0)

    local_pass(ncc, refs_c, 0)
    local_pass(ncx, refs_x, ncc)

    zero = jnp.zeros((hd, hd), F32)

    def ctx_body(i, carry):
        sf, sb = carry
        _, sf = chunk_seq(i, sf, 0)
        _, sb = chunk_seq(ncc - 1 - i, sb, 1)
        return sf, sb

    carry = lax.fori_loop(0, ncc, ctx_body, (zero, zero))
    o_ref[...] = jnp.zeros_like(o_ref)

    def x_body(i, carry):
        sf, sb = carry
        of, sf = chunk_seq(ncc + i, sf, 0)
        o_ref[0, pl.ds(pl.multiple_of(i * cs, cs), cs), :] += of
        ib = ncx - 1 - i
        ob, sb = chunk_seq(ncc + ib, sb, 1)
        o_ref[0, pl.ds(pl.multiple_of(ib * cs, cs), cs), :] += ob
        return sf, sb

    lax.fori_loop(0, ncx, x_body, carry)


def _gdn(qkvx, abx, qkvc, abc, a_log, dt_bias, n_heads, head_dim):
    b, l, _ = qkvx.shape
    lc = qkvc.shape[1]
    cs = GDN_CHUNK
    ncx, ncc = l // cs, lc // cs
    ng = 2 * n_heads
    p_l = jnp.zeros((8, LANES), F32)
    p_l = p_l.at[0, :ng].set(a_log.reshape(ng)).at[1, :ng].set(dt_bias.reshape(ng))

    def col(off):
        return lambda i, h: (i, 0, off + h)

    full = lambda i, h: (i, 0, 0)
    const = lambda i, h: (0, 0)
    hb = head_dim
    return pl.pallas_call(
        functools.partial(_gdn_kernel, n_heads=n_heads, ncx=ncx, ncc=ncc),
        grid=(b, n_heads),
        in_specs=[pl.BlockSpec((1, l, hb), col(0)), pl.BlockSpec((1, l, hb), col(n_heads)),
                  pl.BlockSpec((1, l, hb), col(2 * n_heads)),
                  pl.BlockSpec((1, l, LANES), full),
                  pl.BlockSpec((1, lc, hb), col(0)), pl.BlockSpec((1, lc, hb), col(n_heads)),
                  pl.BlockSpec((1, lc, hb), col(2 * n_heads)),
                  pl.BlockSpec((1, lc, LANES), full),
                  pl.BlockSpec((8, LANES), const)],
        out_specs=pl.BlockSpec((1, l, hb), col(0)),
        out_shape=jax.ShapeDtypeStruct((b, l, n_heads * head_dim), F32),
        scratch_shapes=[pltpu.VMEM((2, l, hb), F32), pltpu.VMEM((2, l, hb), F32),
                        pltpu.VMEM((2, lc, hb), F32), pltpu.VMEM((2, lc, hb), F32),
                        pltpu.VMEM((2, l + lc, hb), F32), pltpu.VMEM((2, l + lc, hb), BF16),
                        pltpu.VMEM((2, l + lc, hb), BF16), pltpu.VMEM((2, l + lc, hb), BF16),
                        pltpu.VMEM((2, l + lc, cs), BF16), pltpu.VMEM((2, (ncx + ncc) * 8, hb), F32)],
        compiler_params=_cparams(("parallel", "arbitrary")),
        name="gdn",
    )(qkvx, qkvx, qkvx, abx, qkvc, qkvc, qkvc, abc, p_l)


def _postmix_kernel(x_ref, y_ref, o_ref, z_ref, mod_ref, gn_ref, gpost_ref, wglu_ref, wout_ref, out_ref,
                    *, n_heads, head_dim, s5_w):
    s = _gelu(jnp.concatenate([y_ref[j].reshape(x_ref.shape[1], LANES) for j in range(y_ref.shape[0])], axis=1))
    s = s * _sigmoid(_dot(s.astype(BF16), wglu_ref[...]))
    o = o_ref[0]
    z = z_ref[0].astype(F32)
    parts = []
    for h in range(n_heads):
        sl = slice(h * head_dim, (h + 1) * head_dim)
        zh = z[:, sl]
        parts.append(_rms(o[:, sl], gn_ref[...]) * (zh * _sigmoid(zh)))
    gd = jnp.concatenate(parts, axis=-1)
    mixed = _dot(s.astype(BF16), wout_ref[0:s5_w, :]) + _dot(gd.astype(BF16), wout_ref[s5_w:, :])
    gate = mod_ref[0][2:3]
    out_ref[0] = x_ref[0] + gate * _rms(mixed, gpost_ref[...])


def _postmix(x, y, o, z, mod, gdn_norm, g_post, w_glu, w_out, n_heads, head_dim):
    b, l, d = x.shape
    s5_w = y.shape[0] * y.shape[-1]
    gw = o.shape[-1]
    tm = min(512, l)
    tc = S5_CHUNK
    tok = lambda i, t: (i, t, 0)
    const = lambda i, t: (0, 0)
    return pl.pallas_call(
        functools.partial(_postmix_kernel, n_heads=n_heads, head_dim=head_dim, s5_w=s5_w),
        grid=(b, l // tm),
        in_specs=[pl.BlockSpec((1, tm, d), tok), pl.BlockSpec((s5_w // LANES, tm // tc, None, tc, LANES), lambda i, t: (0, t, i, 0, 0)),
                  pl.BlockSpec((1, tm, gw), tok), pl.BlockSpec((1, tm, gw), tok),
                  pl.BlockSpec((1,) + mod.shape[1:], lambda i, t: (i, 0, 0)),
                  pl.BlockSpec((1, head_dim), const), pl.BlockSpec((1, d), const),
                  pl.BlockSpec((s5_w, s5_w), const), pl.BlockSpec((s5_w + gw, d), const)],
        out_specs=pl.BlockSpec((1, tm, d), tok),
        out_shape=jax.ShapeDtypeStruct((b, l, d), F32),
        compiler_params=_cparams(("parallel", "parallel")),
        name="postmix",
    )(x, y, o, z, mod, gdn_norm, g_post, w_glu, w_out)


def _peer_prep_kernel(x_ref, mod_ref, g_ref, wq_ref, keys_ref, ht_ref, r2_ref, e2_ref, n1_ref, e1_ref,
                      s1_ref, s2_ref, lv1_ref, lv2_ref, *, n_keys, topk, tt):
    @pl.when(pl.program_id(1) == 0)
    def _():
        m = mod_ref[0]
        h = _rms(x_ref[0], g_ref[...]) * (1.0 + m[4:5]) + m[3:4]
        ht_ref[...] = h.T.astype(BF16)

    half = keys_ref.shape[-1]
    ht = ht_ref[...]
    q1 = _dot(wq_ref[0:half, :], ht).astype(BF16)
    q2 = _dot(wq_ref[half:2 * half, :], ht).astype(BF16)
    s1_ref[...] = _dot(keys_ref[0, 0], q1)
    s2_ref[...] = _dot(keys_ref[0, 1], q2)
    kf = float(topk)

    def tile(t, _):
        cols = pl.ds(pl.multiple_of(t * LANES, LANES), LANES)
        s1, s2 = s1_ref[:, cols], s2_ref[:, cols]

        def lv_body(q, carry):
            c1, c2, rank2 = carry
            m1 = jnp.max(c1, axis=0, keepdims=True)
            m2 = jnp.max(c2, axis=0, keepdims=True)
            lv1_ref[pl.ds(q, 1), :] = m1
            lv2_ref[pl.ds(q, 1), :] = m2
            h1 = c1 == jnp.where(m1 > NEG, m1, -NEG)
            h2 = c2 == jnp.where(m2 > NEG, m2, -NEG)
            rank2 = jnp.where(h2, jnp.asarray(q, F32), rank2)
            return jnp.where(h1, NEG, c1), jnp.where(h2, NEG, c2), rank2

        _, _, rank2 = lax.fori_loop(0, topk, lv_body, (s1, s2, jnp.full(s2.shape, kf, F32)))
        a_mat, b_mat = lv1_ref[...], lv2_ref[...]
        a0, b0 = a_mat[0:1], b_mat[0:1]
        sub = topk // 2
        cand = jnp.concatenate([a_mat[0:1] + b_mat] + [a_mat[p:p + 1] + b_mat[0:sub] for p in range(1, sub)]
                               + [a_mat[sub:] + b0], axis=0)

        def tau_body(_, carry):
            cand, cnt, tau = carry
            mx = jnp.max(cand, axis=0, keepdims=True)
            hit = cand == mx
            tau = jnp.where(cnt < kf, mx, tau)
            cnt = cnt + jnp.sum(jnp.where(hit, 1.0, 0.0), axis=0, keepdims=True)
            return jnp.where(hit, NEG, cand), cnt, tau

        _, _, tau = lax.fori_loop(0, topk, tau_body, (cand, jnp.zeros_like(a0), jnp.full_like(a0, NEG)))

        def n_body(q, carry):
            n1, c2 = carry
            b = lv2_ref[pl.ds(q, 1), :]
            n1 = n1 + jnp.where(s1 + b >= tau, 1.0, 0.0)
            c2 = c2 + jnp.where(a_mat + b >= tau, jnp.exp(b - b0), 0.0)
            return n1, c2

        n1, c2 = lax.fori_loop(0, topk, n_body, (jnp.zeros_like(s1), jnp.zeros_like(a_mat)))
        z = jnp.sum(jnp.exp(a_mat - a0) * c2, axis=0, keepdims=True)
        r2_ref[:, cols] = rank2.astype(BF16)
        e2_ref[:, cols] = (jnp.exp(s2 - b0) / z).astype(BF16)
        n1_ref[0, :, cols] = n1
        e1_ref[0, :, cols] = jnp.exp(s1 - a0)
        return 0

    lax.fori_loop(0, tt // LANES, tile, 0)


def _peer_main_kernel(ht_ref, u_ref, vt_ref, r2_ref, e2_ref, n1_ref, e1_ref, x_ref, mod_ref, g_ref,
                      out_ref, acc_ref, hs0_ref, hs1_ref, act0_ref, act1_ref, r2s_ref, e2s_ref, fence_ref,
                      *, n_heads, n_keys, eb, tt, n_blocks):
    k = pl.program_id(1)
    nsub = eb // n_keys
    pk = BF16_ROWS
    grp = n_keys // pk

    @pl.when(k == 0)
    def _():
        acc_ref[...] = jnp.zeros_like(acc_ref)
        hs1_ref[...] = jnp.zeros_like(hs1_ref)
        act0_ref[...] = jnp.zeros_like(act0_ref)
        act1_ref[...] = jnp.zeros_like(act1_ref)
        r2s_ref[...] = r2_ref[...]
        e2s_ref[...] = e2_ref[...]

    def gate_tiles(hs_ref, act_ref, blk, ii):
        i = blk * nsub + ii
        nrows, erows = n1_ref[i], e1_ref[i]
        rs = slice(ii * n_keys, (ii + 1) * n_keys)
        for tc in range(tt // LANES):
            cs_ = slice(tc * LANES, (tc + 1) * LANES)
            gsum = jnp.zeros((n_keys, LANES), BF16)
            for hd in range(n_heads):
                nb, ebc = (jnp.concatenate([jnp.broadcast_to(r[hd:hd + 1, cs_], (pk, LANES)).astype(BF16)] * grp, 0)
                           for r in (nrows, erows))
                ks = slice(hd * n_keys, (hd + 1) * n_keys)
                r2, e2 = r2s_ref[ks, cs_], e2s_ref[ks, cs_]
                gsum = gsum + jnp.where(r2 < nb, e2, jnp.zeros_like(e2)) * ebc
            act_ref[rs, cs_] = _gelu(hs_ref[rs, cs_]).astype(BF16) * gsum

    d = ht_ref.shape[0]
    kc = d // nsub
    ec = eb // 2

    def half_step(u_rows, hs_w, hs_r, act_w, act_r, blk):
        for ii in range(nsub):
            u_part = pltpu.bitcast(u_ref[u_rows // 2:(u_rows + eb) // 2, ii * kc:(ii + 1) * kc], BF16)
            part = _dot(u_part, ht_ref[ii * kc:(ii + 1) * kc, :])
            if ii == 0:
                hs_w[...] = part
            else:
                hs_w[...] += part
            gate_tiles(hs_r, act_w, blk, ii)
            if ii % 2 == 0:
                j = ii // 2
                vt_part = pltpu.bitcast(vt_ref[:, u_rows + j * ec:u_rows + (j + 1) * ec], BF16)
                acc_ref[...] += _dot(vt_part, act_r[j * ec:(j + 1) * ec, :])

    half_step(0, hs0_ref, hs1_ref, act1_ref, act0_ref, jnp.maximum(2 * k - 1, 0))
    pl.semaphore_signal(fence_ref, 1)
    pl.semaphore_wait(fence_ref, 1)
    half_step(eb, hs1_ref, hs0_ref, act0_ref, act1_ref, jnp.minimum(2 * k, n_blocks - 1))

    @pl.when(k == pl.num_programs(1) - 1)
    def _():
        o = acc_ref[...].T
        out_ref[0] = x_ref[0] + mod_ref[0][5:6] * _rms(o, g_ref[...])


def _peer(x1, mod, g_pre, g_post, w_q, keys, exp_u, exp_v):
    b, l, d = x1.shape
    n_heads, _, n_keys, half = keys.shape
    n_exp = exp_u.shape[0]
    ntok = b * l
    tt = min(512, l)
    eb = 512
    tpb = l // tt
    wq_t = w_q.T.astype(BF16)
    keys_b = keys.astype(BF16)
    u_b = _pack_rows(exp_u.astype(BF16))
    vt_b = _pack_rows(exp_v.T.astype(BF16))
    tokb = lambda t, h: (t // tpb, t % tpb, 0)
    modb = lambda t, h: (t // tpb, 0, 0)
    hmap = lambda t, h: (h, 0, t)
    hshape = jax.ShapeDtypeStruct((n_heads, n_keys, ntok), F32)
    hshape_b = jax.ShapeDtypeStruct((n_heads * n_keys, ntok), BF16)
    ht, r2, e2, n1, e1 = pl.pallas_call(
        functools.partial(_peer_prep_kernel, n_keys=n_keys, topk=PEER_TOPK, tt=tt),
        grid=(ntok // tt, n_heads),
        in_specs=[pl.BlockSpec((1, tt, d), tokb), pl.BlockSpec((1,) + mod.shape[1:], modb),
                  pl.BlockSpec((1, d), lambda t, h: (0, 0)),
                  pl.BlockSpec((2 * half, d), lambda t, h: (h, 0)),
                  pl.BlockSpec((1, 2, n_keys, half), lambda t, h: (h, 0, 0, 0))],
        out_specs=[pl.BlockSpec((d, tt), lambda t, h: (0, t))] + [pl.BlockSpec((n_keys, tt), lambda t, h: (h, t))] * 2
                  + [pl.BlockSpec((1, n_keys, tt), hmap)] * 2,
        out_shape=[jax.ShapeDtypeStruct((d, ntok), BF16), hshape_b, hshape_b, hshape, hshape],
        scratch_shapes=[pltpu.VMEM((n_keys, tt), F32), pltpu.VMEM((n_keys, tt), F32),
                        pltpu.VMEM((PEER_TOPK, LANES), F32), pltpu.VMEM((PEER_TOPK, LANES), F32)],
        compiler_params=_cparams(("parallel", "arbitrary")),
        name="peer_prep",
    )(x1, mod, g_pre, wq_t, keys_b)

    n_blocks = n_exp // eb
    n_pairs = n_blocks // 2
    tokb2 = lambda t, k: (t // tpb, t % tpb, 0)
    hmap2 = lambda t, k: (0, 0, t)
    return pl.pallas_call(
        functools.partial(_peer_main_kernel, n_heads=n_heads, n_keys=n_keys, eb=eb, tt=tt, n_blocks=n_blocks),
        grid=(ntok // tt, n_pairs + 1),
        in_specs=[pl.BlockSpec((d, tt), lambda t, k: (0, t)),
                  pl.BlockSpec((eb, d), lambda t, k: (jnp.minimum(k, n_pairs - 1), 0)),
                  pl.BlockSpec((d // 2, 2 * eb), lambda t, k: (0, jnp.maximum(k - 1, 0)))]
                 + [pl.BlockSpec((n_heads * n_keys, tt), lambda t, k: (0, t))] * 2
                 + [pl.BlockSpec((n_keys, n_heads, tt), hmap2)] * 2
                 + [pl.BlockSpec((1, tt, d), tokb2),
                    pl.BlockSpec((1,) + mod.shape[1:], lambda t, k: (t // tpb, 0, 0)),
                    pl.BlockSpec((1, d), lambda t, k: (0, 0))],
        out_specs=pl.BlockSpec((1, tt, d), tokb2),
        out_shape=jax.ShapeDtypeStruct((b, l, d), F32),
        scratch_shapes=[pltpu.VMEM((d, tt), F32), pltpu.VMEM((eb, tt), F32), pltpu.VMEM((eb, tt), F32),
                        pltpu.VMEM((eb, tt), BF16), pltpu.VMEM((eb, tt), BF16),
                        pltpu.VMEM((n_heads * n_keys, tt), BF16), pltpu.VMEM((n_heads * n_keys, tt), BF16),
                        pltpu.SemaphoreType.REGULAR],
        compiler_params=_cparams(("parallel", "arbitrary")),
        name="peer_main",
    )(ht, u_b, vt_b, r2, e2, n1.transpose(1, 0, 2), e1.transpose(1, 0, 2), x1, mod, g_post)


def kernel(x, c, ctx, c_ctx, w_mod, b_mod, g_pre_mix, g_post_mix, g_pre_ffn, g_post_ffn, w_in, w_out,
           s5_lam_re, s5_lam_im, s5_log_dt, s5_b_re, s5_b_im, s5_c_re, s5_c_im, s5_d, s5_w_glu,
           gdn_conv, gdn_a_log, gdn_dt_bias, gdn_norm, peer_w_q, peer_keys, peer_u, peer_v):
    assert w_mod.shape[0] == 1, "single layer: the context stream only feeds the latent scans"
    b, l, d = x.shape
    lc = ctx.shape[1]
    n_heads = gdn_a_log.shape[-1]
    head_dim = gdn_norm.shape[-1]
    gdn_w = n_heads * head_dim
    s5_w = s5_d.shape[-1]
    n_gate = 2 * 2 * n_heads

    n_cond = -(-(b + 1) // 8) * 8
    cond = jnp.concatenate([c, c_ctx[None], jnp.zeros((n_cond - b - 1, d), F32)], axis=0)
    mod = _adaln(cond, w_mod[0], b_mod[0]).reshape(n_cond, 6, d)

    n_proj = w_in.shape[-1]
    w_pad = jnp.pad(w_in[0], ((0, 0), (0, -n_proj % LANES))).astype(BF16)
    g1 = g_pre_mix[0][None]
    ux, qkvx, zx, abx = _premix(x, mod[:b], True, g1, w_pad, s5_w, 3 * gdn_w, gdn_w)
    uc, qkvc, _, abc = _premix(ctx, mod[b:b + 1], False, g1, w_pad, s5_w, 3 * gdn_w, gdn_w)
    assert abx.shape[-1] == LANES and n_gate <= LANES

    mats = _s5_matrices(s5_lam_re[0], s5_lam_im[0], s5_log_dt[0], s5_b_re[0], s5_b_im[0],
                        s5_c_re[0], s5_c_im[0], s5_d[0])
    y_s5 = _s5(ux, uc, mats)

    qkvx = _gdn_conv(qkvx, gdn_conv[0], GRID_W, l // GRID_W > 1, n_heads, head_dim)
    qkvc = _gdn_conv(qkvc, gdn_conv[0], lc, False, n_heads, head_dim)
    o_gdn = _gdn(qkvx, abx, qkvc, abc, gdn_a_log[0], gdn_dt_bias[0], n_heads, head_dim)

    x1 = _postmix(x, y_s5, o_gdn, zx, mod[:b], gdn_norm[0][None], g_post_mix[0][None],
                  s5_w_glu[0].astype(BF16), w_out[0].astype(BF16), n_heads, head_dim)
    return _peer(x1, mod[:b], g_pre_ffn[0][None], g_post_ffn[0][None],
                 peer_w_q[0], peer_keys[0], peer_u[0], peer_v[0])
```

```python
import functools

import jax
import jax.numpy as jnp
from jax import lax
from jax.experimental import pallas as pl
from jax.experimental.pallas import tpu as pltpu

F32 = jnp.float32
BF16 = jnp.bfloat16
HI = lax.Precision.HIGHEST

EPS = 1e-6
GRID_W = 64
GDN_CHUNK = 64
S5_CHUNK = 8
PEER_TOPK = 16
LANES = 128
BF16_ROWS = 16
NEG = -1e30
VMEM_LIMIT = 56 * 1024 * 1024


def _cparams(sem):
    return pltpu.CompilerParams(dimension_semantics=sem, vmem_limit_bytes=VMEM_LIMIT)


def _sigmoid(x):
    return 1.0 / (1.0 + jnp.exp(-x))


def _softplus(x):
    return jnp.maximum(x, 0.0) + jnp.log(1.0 + jnp.exp(-jnp.abs(x)))


def _gelu(x):
    return 0.5 * x * (1.0 + lax.erf(x * (2.0 ** -0.5)))


def _rms(x, g):
    return x * lax.rsqrt(jnp.mean(x * x, axis=-1, keepdims=True) + EPS) * g


def _dot(a, b, dims=None, precision=None):
    if dims is None:
        dims = (((a.ndim - 1,), (0,)), ((), ()))
    return lax.dot_general(a, b, dims, precision=precision, preferred_element_type=F32)


def _pack_kernel(x_ref, o_ref, *, transpose):
    x = x_ref[...]
    x = x.T if transpose else x
    o_ref[...] = pltpu.bitcast(x.astype(BF16), jnp.uint32)


def _pack_rows(x, transpose):
    n, m = x.shape
    tn = 512
    if transpose:
        out_shape, out_spec = (m // 2, n), pl.BlockSpec((m // 2, tn), lambda i: (0, i))
    else:
        out_shape, out_spec = (n // 2, m), pl.BlockSpec((tn // 2, m), lambda i: (i, 0))
    return pl.pallas_call(
        functools.partial(_pack_kernel, transpose=transpose),
        grid=(n // tn,),
        in_specs=[pl.BlockSpec((tn, m), lambda i: (i, 0))],
        out_specs=out_spec,
        out_shape=jax.ShapeDtypeStruct(out_shape, jnp.uint32),
        compiler_params=_cparams(("parallel",)),
        name="pack_table",
    )(x)


_NT = (((1,), (1,)), ((), ()))
_TN = (((0,), (0,)), ((), ()))


def _adaln_kernel(c_ref, w_ref, b_ref, o_ref):
    c = c_ref[...]
    o_ref[...] = _dot(c * _sigmoid(c), w_ref[...], precision=HI) + b_ref[...]


def _adaln(cond, w_mod, b_mod):
    n, d = cond.shape
    m = w_mod.shape[1]
    tn = 512
    return pl.pallas_call(
        _adaln_kernel,
        grid=(m // tn,),
        in_specs=[pl.BlockSpec((n, d), lambda j: (0, 0)),
                  pl.BlockSpec((d, tn), lambda j: (0, j)),
                  pl.BlockSpec((1, tn), lambda j: (0, j))],
        out_specs=pl.BlockSpec((n, tn), lambda j: (0, j)),
        out_shape=jax.ShapeDtypeStruct((n, m), F32),
        compiler_params=_cparams(("arbitrary",)),
        name="adaln",
    )(cond, w_mod, b_mod.reshape(1, m))


def _premix_kernel(x_ref, mod_ref, g_ref, w_ref, u_ref, qkv_ref, z_ref, ab_ref, *, n_u, n_qkv, n_z):
    m = mod_ref[0]
    h = _rms(x_ref[0], g_ref[...]) * (1.0 + m[1:2]) + m[0:1]
    p = _dot(h.astype(BF16), w_ref[...])
    for j in range(u_ref.shape[0]):
        u_ref[j] = p[:, j * LANES:(j + 1) * LANES].reshape(u_ref.shape[1:])
    qkv_ref[0] = p[:, n_u:n_u + n_qkv]
    z_ref[0] = p[:, n_u + n_qkv:n_u + n_qkv + n_z].astype(BF16)
    ab_ref[0] = p[:, n_u + n_qkv + n_z:]


def _premix(x, mod, per_batch, g, w_pad, n_u, n_qkv, n_z):
    b, l, d = x.shape
    tm = min(512, l)
    tc = S5_CHUNK
    n_all = w_pad.shape[1]
    n_ab = n_all - n_u - n_qkv - n_z
    mod_map = (lambda i, t: (i, 0, 0)) if per_batch else (lambda i, t: (0, 0, 0))
    tok = lambda i, t: (i, t, 0)
    return pl.pallas_call(
        functools.partial(_premix_kernel, n_u=n_u, n_qkv=n_qkv, n_z=n_z),
        grid=(b, l // tm),
        in_specs=[pl.BlockSpec((1, tm, d), tok),
                  pl.BlockSpec((1,) + mod.shape[1:], mod_map),
                  pl.BlockSpec((1, d), lambda i, t: (0, 0)),
                  pl.BlockSpec((d, n_all), lambda i, t: (0, 0))],
        out_specs=[pl.BlockSpec((n_u // LANES, tm // tc, None, tc, LANES), lambda i, t: (0, t, i, 0, 0)),
                   pl.BlockSpec((1, tm, n_qkv), tok),
                   pl.BlockSpec((1, tm, n_z), tok), pl.BlockSpec((1, tm, n_ab), tok)],
        out_shape=[jax.ShapeDtypeStruct((n_u // LANES, l // tc, b, tc, LANES), F32), jax.ShapeDtypeStruct((b, l, n_qkv), F32),
                   jax.ShapeDtypeStruct((b, l, n_z), BF16), jax.ShapeDtypeStruct((b, l, n_ab), F32)],
        compiler_params=_cparams(("parallel", "parallel")),
        name="premix",
    )(x, mod, g, w_pad)


def _s5_matrices(lam_re, lam_im, log_dt, b_re, b_im, c_re, c_im, d_skip):
    t = S5_CHUNK
    n_dirs, g, p = lam_re.shape
    cg = b_re.shape[-1]
    dt = jnp.exp(log_dt)[..., None]
    lr, li = lam_re, lam_im
    den = lr * lr + li * li
    mag = jnp.exp(lr * dt)
    lb_re, lb_im = mag * jnp.cos(li * dt), mag * jnp.sin(li * dt)
    nr = lb_re - 1.0
    cr = (nr * lr + lb_im * li) / den
    ci = (lb_im * lr - nr * li) / den
    bb_re = cr[..., None] * b_re - ci[..., None] * b_im
    bb_im = cr[..., None] * b_im + ci[..., None] * b_re
    taus = jnp.arange(t + 1, dtype=F32)[:, None, None, None]
    pmag = jnp.exp(taus * (lr * dt))
    pw_re = pmag * jnp.cos(taus * (li * dt))
    pw_im = pmag * jnp.sin(taus * (li * dt))
    ein = functools.partial(jnp.einsum, precision=HI)
    lbr = pw_re[..., None] * bb_re - pw_im[..., None] * bb_im
    lbi = pw_re[..., None] * bb_im + pw_im[..., None] * bb_re
    kk = ein('dgcp,tdgpe->tdgce', c_re, lbr) - ein('dgcp,tdgpe->tdgce', c_im, lbi)
    er = c_re[None] * pw_re[:, :, :, None, :] - c_im[None] * pw_im[:, :, :, None, :]
    ei = c_re[None] * pw_im[:, :, :, None, :] + c_im[None] * pw_re[:, :, :, None, :]

    s_idx = jnp.arange(t)
    lag = s_idx[None, :] - s_idx[:, None]
    kf = kk[:, 0][jnp.clip(lag, 0, t)]
    kb = kk[:, 1][jnp.clip(-lag, 0, t)]
    m_full = (jnp.where((lag >= 0)[:, :, None, None, None], kf, 0.0)
              + jnp.where((lag <= 0)[:, :, None, None, None], kb, 0.0))
    eye_t = jnp.eye(t, dtype=F32)[:, :, None, None, None]
    eye_c = jnp.eye(cg, dtype=F32)[None, None, None]
    m_full = m_full + eye_t * eye_c * d_skip.reshape(g, cg)[None, None, :, :, None]
    p_parts = (lbr[:t, 0][::-1], lbi[:t, 0][::-1], lbr[:t, 1], lbi[:t, 1])
    r_parts = (er[1:, 0], -ei[1:, 0], er[1:, 1][::-1], -ei[1:, 1][::-1])

    gw = LANES // cg
    n8 = g // gw
    eye = jnp.eye(gw, dtype=F32)
    m6 = m_full.reshape(t, t, n8, gw, cg, cg).transpose(2, 0, 3, 5, 1, 4)
    mm = (m6[:, :, :, :, :, None, :] * eye[None, None, :, None, None, :, None]).reshape(n8, t * LANES, t * LANES)

    def p_full(a):
        a6 = a.reshape(t, n8, gw, p, cg).transpose(1, 0, 2, 4, 3)
        return a6[:, :, :, :, None, :] * eye[None, None, :, None, :, None]

    pq = jnp.stack([p_full(x) for x in p_parts], axis=4).reshape(n8, t * LANES, 4 * gw * p)

    def r_full(a):
        a6 = a.reshape(t, n8, gw, cg, p).transpose(1, 2, 4, 0, 3)
        return a6[:, :, :, :, None, :] * eye[None, :, None, None, :, None]

    rr = jnp.stack([r_full(x) for x in r_parts], axis=1).reshape(n8, 4 * gw * p, t * LANES)
    dec = jnp.stack([x.reshape(n8, gw * p) for x in (pw_re[t, 0], pw_im[t, 0], pw_re[t, 1], pw_im[t, 1])], axis=1)
    dec = jnp.concatenate([dec, jnp.zeros_like(dec)], axis=1)
    return pq.astype(BF16), mm.astype(BF16), rr.astype(BF16), dec


def _chunk_rows(ref, nb, t):
    n = ref.shape[0]
    return jnp.concatenate([ref[:, :, s, :].reshape(n * nb, LANES) for s in range(t)], axis=1).astype(BF16)


def _s5_state_kernel(x_ref, c_ref, pq_ref, dec_ref, sin_ref, loc_ref, locc_ref, *, nb, t, nct, ncx, ncc, sw):
    rt = pl.program_id(1)
    rows = nct * nb
    loc_ref[pl.ds(pl.multiple_of(rt * rows, rows), rows), :] = _dot(_chunk_rows(x_ref, nb, t), pq_ref[0])

    @pl.when(rt == 0)
    def _():
        locc_ref[...] = _dot(_chunk_rows(c_ref, nb, t), pq_ref[0])

    @pl.when(rt == pl.num_programs(1) - 1)
    def _():
        dec = dec_ref[0]
        far, fai, bar, bai = (jnp.broadcast_to(dec[i:i + 1], (nb, sw)) for i in range(4))

        def step(state, ar, ai, ref, k, c0):
            re, im = state
            r = pl.ds(pl.multiple_of(k * nb, nb), nb)
            return ar * re - ai * im + ref[r, c0:c0 + sw], ar * im + ai * re + ref[r, c0 + sw:c0 + 2 * sw]

        zero = jnp.zeros((nb, sw), F32)

        def ctx_body(i, carry):
            f, bk = carry
            return step(f, far, fai, locc_ref, i, 0), step(bk, bar, bai, locc_ref, ncc - 1 - i, 2 * sw)

        carry = lax.fori_loop(0, ncc, ctx_body, ((zero, zero), (zero, zero)))

        def pair_store(k0, c0, s_lo, s_hi):
            r = pl.ds(pl.multiple_of(k0 * nb, 2 * nb), 2 * nb)
            sin_ref[0, r, c0:c0 + sw] = jnp.concatenate([s_lo[0], s_hi[0]], axis=0).astype(BF16)
            sin_ref[0, r, c0 + sw:c0 + 2 * sw] = jnp.concatenate([s_lo[1], s_hi[1]], axis=0).astype(BF16)

        def x_body(j, carry):
            f0, b0 = carry
            f1 = step(f0, far, fai, loc_ref, 2 * j, 0)
            pair_store(2 * j, 0, f0, f1)
            kb = ncx - 2 - 2 * j
            b1 = step(b0, bar, bai, loc_ref, kb + 1, 2 * sw)
            pair_store(kb, 2 * sw, b1, b0)
            return step(f1, far, fai, loc_ref, 2 * j + 1, 0), step(b1, bar, bai, loc_ref, kb, 2 * sw)

        lax.fori_loop(0, ncx // 2, x_body, carry)


def _s5_out_kernel(x_ref, sin_ref, mm_ref, rr_ref, y_ref, *, nb, t):
    n = x_ref.shape[0]
    y = _dot(_chunk_rows(x_ref, nb, t), mm_ref[0]) + _dot(sin_ref[0], rr_ref[0])
    for s in range(t):
        y_ref[:, :, s, :] = y[:, s * LANES:(s + 1) * LANES].reshape(n, nb, LANES)


def _s5(ux, uc, mats):
    pq, mm, rr, dec = mats
    n8, ncx, nb, t, _ = ux.shape
    ncc = uc.shape[1]
    sw = dec.shape[-1]
    assert nb % 8 == 0 and t == S5_CHUNK and ncx % 2 == 0 and n8 == pq.shape[0]
    nct = min(64, ncx)
    ux5, uc5 = ux, uc
    kdim = t * LANES
    xblk = pl.BlockSpec((None, nct, nb, t, LANES), lambda g, r: (g, r, 0, 0, 0))
    wmap = lambda g, r: (g, 0, 0)
    sin = pl.pallas_call(
        functools.partial(_s5_state_kernel, nb=nb, t=t, nct=nct, ncx=ncx, ncc=ncc, sw=sw),
        grid=(n8, ncx // nct),
        in_specs=[xblk, pl.BlockSpec((None, ncc, nb, t, LANES), lambda g, r: (g, 0, 0, 0, 0)),
                  pl.BlockSpec((1, kdim, 4 * sw), wmap), pl.BlockSpec((1, 8, sw), wmap)],
        out_specs=pl.BlockSpec((1, ncx * nb, 4 * sw), wmap),
        out_shape=jax.ShapeDtypeStruct((n8, ncx * nb, 4 * sw), BF16),
        scratch_shapes=[pltpu.VMEM((ncx * nb, 4 * sw), F32), pltpu.VMEM((ncc * nb, 4 * sw), F32)],
        compiler_params=_cparams(("parallel", "arbitrary")),
        name="s5_state",
    )(ux5, uc5, pq, dec)
    y = pl.pallas_call(
        functools.partial(_s5_out_kernel, nb=nb, t=t),
        grid=(n8, ncx // nct),
        in_specs=[xblk, pl.BlockSpec((1, nct * nb, 4 * sw), lambda g, r: (g, r, 0)),
                  pl.BlockSpec((1, kdim, kdim), wmap), pl.BlockSpec((1, 4 * sw, kdim), wmap)],
        out_specs=xblk,
        out_shape=jax.ShapeDtypeStruct((n8, ncx, nb, t, LANES), F32),
        compiler_params=_cparams(("parallel", "parallel")),
        name="s5_out",
    )(ux5, sin, mm, rr)
    return y


_CONV_PAD = 72


def _gdn_conv_kernel(x_ref, w_ref, o_ref, pad_ref, *, length, cols, two_d, n_norm, n_q, q_scale):
    cb = pl.program_id(1)
    pad = _CONV_PAD
    zeros = jnp.zeros((pad, LANES), F32)
    pad_ref[0:pad, :] = zeros
    pad_ref[pad + length:pad + length + pad, :] = zeros
    pad_ref[pad:pad + length, :] = x_ref[0]
    rc = min(256, length)
    for r0 in range(0, length, rc):
        col = lax.rem(lax.broadcasted_iota(jnp.int32, (rc, LANES), 0) + r0, cols)
        acc = jnp.zeros((rc, LANES), F32)
        for dr in ((0, 1, 2) if two_d else (1,)):
            for dc in range(3):
                off = (dr - 1) * cols + (dc - 1)
                term = pad_ref[pad + r0 + off:pad + r0 + off + rc, :] * w_ref[dr * 3 + dc:dr * 3 + dc + 1, :]
                if dc == 0:
                    term = jnp.where(col >= 1, term, 0.0)
                if dc == 2:
                    term = jnp.where(col <= cols - 2, term, 0.0)
                acc = acc + term
        y = acc * _sigmoid(acc)
        nrm = y * lax.rsqrt(jnp.sum(y * y, axis=-1, keepdims=True) + EPS)
        nrm = nrm * jnp.where(cb < n_q, q_scale, 1.0)
        o_ref[0, r0:r0 + rc, :] = jnp.where(cb < n_norm, nrm, y)


def _gdn_conv(qkv, conv_w, cols, two_d, n_heads, head_dim):
    b, l, ch = qkv.shape
    w9 = jnp.concatenate([conv_w.reshape(9, ch), jnp.zeros((7, ch), F32)], axis=0)
    blk = lambda i, j: (i, 0, j)
    return pl.pallas_call(
        functools.partial(_gdn_conv_kernel, length=l, cols=cols, two_d=two_d,
                          n_norm=2 * n_heads, n_q=n_heads, q_scale=head_dim ** -0.5),
        grid=(b, ch // LANES),
        in_specs=[pl.BlockSpec((1, l, LANES), blk), pl.BlockSpec((16, LANES), lambda i, j: (0, j))],
        out_specs=pl.BlockSpec((1, l, LANES), blk),
        out_shape=jax.ShapeDtypeStruct((b, l, ch), F32),
        scratch_shapes=[pltpu.VMEM((l + 2 * _CONV_PAD, LANES), F32)],
        compiler_params=_cparams(("parallel", "parallel")),
        name="gdn_conv",
    )(qkv, w9)


def _gdn_kernel(qx_ref, kx_ref, vx_ref, abx_ref, qc_ref, kc_ref, vc_ref, abc_ref,
                pl_ref, o_ref, gcx_ref, bcx_ref, gcc_ref, bcc_ref,
                u_ref, w_ref, qd_ref, kd_ref, at_ref, gt_ref, *, n_heads, ncx, ncc):
    cs = GDN_CHUNK
    hd = qx_ref.shape[-1]
    head = pl.program_id(1)
    alog_l, dtb_l = pl_ref[0:1, :], pl_ref[1:2, :]
    lane = lax.broadcasted_iota(jnp.int32, (1, LANES), 1)

    def gate_columns(ab_ref, g_out, b_out, n_rows):
        rc = min(256, n_rows)
        for r0 in range(0, n_rows, rc):
            ab = ab_ref[0, r0:r0 + rc, :]
            g_all = -jnp.exp(alog_l) * _softplus(ab + dtb_l)
            b_all = _sigmoid(ab)
            for dr in range(2):
                j = dr * n_heads + head
                g = jnp.sum(jnp.where(lane == j, g_all, 0.0), axis=-1, keepdims=True)
                bt = jnp.sum(jnp.where(lane == j + 2 * n_heads, b_all, 0.0), axis=-1, keepdims=True)
                g_out[dr, r0:r0 + rc, :] = jnp.broadcast_to(g, (rc, hd))
                b_out[dr, r0:r0 + rc, :] = jnp.broadcast_to(bt, (rc, hd))

    gate_columns(abx_ref, gcx_ref, bcx_ref, ncx * cs)
    gate_columns(abc_ref, gcc_ref, bcc_ref, ncc * cs)

    nblk = 4
    side = nblk * cs
    def block_masks():
        ri = lax.broadcasted_iota(jnp.int32, (side, side), 0)
        ci = lax.broadcasted_iota(jnp.int32, (side, side), 1)
        dist = jnp.where(ri < side // 2, ri - ci, ci - ri)
        dist = jnp.where((ri // cs) == (ci // cs), dist, -1)
        return dist >= 0, dist > 0, jnp.where(ri == ci, 1.0, 0.0)

    def cumsum_blocks(x):
        row_id = lax.broadcasted_iota(jnp.int32, (side, hd), 0)
        pos = row_id % cs
        is_fwd = row_id < side // 2
        s = 1
        while s < cs:
            x = x + jnp.where(is_fwd, jnp.where(pos >= s, pltpu.roll(x, s, 0), 0.0),
                              jnp.where(pos < cs - s, pltpu.roll(x, side - s, 0), 0.0))
            s *= 2
        return x

    def chunks_local(c0, refs, base):
        q_ref, k_ref, v_ref, gcol_ref, bcol_ref = refs
        incl, strict, eye = block_masks()
        rows = pl.ds(pl.multiple_of(c0 * cs, 2 * cs), 2 * cs)
        q2, k2, v2 = q_ref[0, rows, :], k_ref[0, rows, :], v_ref[0, rows, :]
        q, k, v = (jnp.concatenate([t, t], axis=0) for t in (q2, k2, v2))
        gb = jnp.concatenate([gcol_ref[0, rows, :], gcol_ref[1, rows, :]], axis=0)
        bb = jnp.concatenate([bcol_ref[0, rows, :], bcol_ref[1, rows, :]], axis=0)
        gcs = cumsum_blocks(gb)
        gwide = jnp.concatenate([gcs] * (side // hd), axis=1)
        diff = gwide - gwide.T[0:1, :]
        decay = jnp.where(incl, jnp.exp(jnp.where(incl, diff, 0.0)), 0.0)
        kbf = k.astype(BF16)
        kb = k * bb
        lmat = jnp.where(strict, _dot(kb.astype(BF16), kbf, _NT) * decay, 0.0)
        pw = -lmat
        tmat = eye + pw
        for _ in range(cs.bit_length() - 2):
            pw = _dot(pw.astype(BF16), pw.astype(BF16))
            tmat = tmat + _dot(tmat.astype(BF16), pw.astype(BF16))
        g_exp = jnp.exp(gcs)
        rhs = jnp.concatenate([v * bb, kb * g_exp], axis=1).astype(BF16)
        uw = _dot(tmat.astype(BF16), rhs)
        attn = jnp.where(incl, _dot(q.astype(BF16), kbf, _NT) * decay, 0.0)
        lasts = [gcs[(j + 1) * cs - 1:(j + 1) * cs, :] if j < nblk // 2 else gcs[j * cs:j * cs + 1, :] for j in range(nblk)]
        g_last = jnp.concatenate([jnp.broadcast_to(r, (cs, hd)) for r in lasts], axis=0)
        qd = (q * g_exp).astype(BF16)
        kd = (k * jnp.exp(g_last - gcs)).astype(BF16)
        srows = pl.ds(pl.multiple_of((base + c0) * cs, 2 * cs), 2 * cs)
        grows = pl.ds(pl.multiple_of((base + c0) * 8, 16), 16)
        half = side // 2
        for dr in range(2):
            blk = slice(dr * half, (dr + 1) * half)
            u_ref[dr, srows, :] = uw[blk, :hd]
            w_ref[dr, srows, :] = uw[blk, hd:].astype(BF16)
            qd_ref[dr, srows, :] = qd[blk]
            kd_ref[dr, srows, :] = kd[blk]
            at_ref[dr, srows, :] = jnp.concatenate(
                [attn[j * cs:(j + 1) * cs, j * cs:(j + 1) * cs] for j in (2 * dr, 2 * dr + 1)], axis=0).astype(BF16)
            gt_ref[dr, grows, :] = jnp.concatenate(
                [jnp.broadcast_to(jnp.exp(lasts[j]), (8, hd)) for j in (2 * dr, 2 * dr + 1)], axis=0)

    def chunk_seq(cg, state, dr):
        srows = pl.ds(pl.multiple_of(cg * cs, cs), cs)
        sb = state.astype(BF16)
        v_new = u_ref[dr, srows, :] - _dot(w_ref[dr, srows, :], sb)
        vb = v_new.astype(BF16)
        o = _dot(qd_ref[dr, srows, :], sb) + _dot(at_ref[dr, srows, :], vb)
        gt = gt_ref[dr, pl.ds(pl.multiple_of(cg * 8, 8), 8), :][0:1]
        return o, state * gt + _dot(kd_ref[dr, srows, :], vb, _TN)

    refs_c = (qc_ref, kc_ref, vc_ref, gcc_ref, bcc_ref)
    refs_x = (qx_ref, kx_ref, vx_ref, gcx_ref, bcx_ref)

    def local_pass(n, refs, base):
        def body(i, _):
            chunks_local(2 * i, refs, base)
            return 0

        lax.fori_loop(0, n // 2, body, 0)

    local_pass(ncc, refs_c, 0)
    local_pass(ncx, refs_x, ncc)

    zero = jnp.zeros((hd, hd), F32)

    def ctx_body(i, carry):
        sf, sb = carry
        _, sf = chunk_seq(i, sf, 0)
        _, sb = chunk_seq(ncc - 1 - i, sb, 1)
        return sf, sb

    carry = lax.fori_loop(0, ncc, ctx_body, (zero, zero))
    o_ref[...] = jnp.zeros_like(o_ref)

    def x_body(i, carry):
        sf, sb = carry
        of, sf = chunk_seq(ncc + i, sf, 0)
        o_ref[0, pl.ds(pl.multiple_of(i * cs, cs), cs), :] += of
        ib = ncx - 1 - i
        ob, sb = chunk_seq(ncc + ib, sb, 1)
        o_ref[0, pl.ds(pl.multiple_of(ib * cs, cs), cs), :] += ob
        return sf, sb

    lax.fori_loop(0, ncx, x_body, carry)


def _gdn(qkvx, abx, qkvc, abc, a_log, dt_bias, n_heads, head_dim):
    b, l, _ = qkvx.shape
    lc = qkvc.shape[1]
    cs = GDN_CHUNK
    ncx, ncc = l // cs, lc // cs
    assert ncx % 2 == 0 and ncc % 2 == 0, "chunks are processed in pairs"
    ng = 2 * n_heads
    p_l = jnp.zeros((8, LANES), F32)
    p_l = p_l.at[0, :ng].set(a_log.reshape(ng)).at[1, :ng].set(dt_bias.reshape(ng))

    def col(off):
        return lambda i, h: (i, 0, off + h)

    full = lambda i, h: (i, 0, 0)
    const = lambda i, h: (0, 0)
    hb = head_dim
    return pl.pallas_call(
        functools.partial(_gdn_kernel, n_heads=n_heads, ncx=ncx, ncc=ncc),
        grid=(b, n_heads),
        in_specs=[pl.BlockSpec((1, l, hb), col(0)), pl.BlockSpec((1, l, hb), col(n_heads)),
                  pl.BlockSpec((1, l, hb), col(2 * n_heads)),
                  pl.BlockSpec((1, l, LANES), full),
                  pl.BlockSpec((1, lc, hb), col(0)), pl.BlockSpec((1, lc, hb), col(n_heads)),
                  pl.BlockSpec((1, lc, hb), col(2 * n_heads)),
                  pl.BlockSpec((1, lc, LANES), full),
                  pl.BlockSpec((8, LANES), const)],
        out_specs=pl.BlockSpec((1, l, hb), col(0)),
        out_shape=jax.ShapeDtypeStruct((b, l, n_heads * head_dim), F32),
        scratch_shapes=[pltpu.VMEM((2, l, hb), F32), pltpu.VMEM((2, l, hb), F32),
                        pltpu.VMEM((2, lc, hb), F32), pltpu.VMEM((2, lc, hb), F32),
                        pltpu.VMEM((2, l + lc, hb), F32), pltpu.VMEM((2, l + lc, hb), BF16),
                        pltpu.VMEM((2, l + lc, hb), BF16), pltpu.VMEM((2, l + lc, hb), BF16),
                        pltpu.VMEM((2, l + lc, cs), BF16), pltpu.VMEM((2, (ncx + ncc) * 8, hb), F32)],
        compiler_params=_cparams(("parallel", "arbitrary")),
        name="gdn",
    )(qkvx, qkvx, qkvx, abx, qkvc, qkvc, qkvc, abc, p_l)


def _postmix_kernel(x_ref, y_ref, o_ref, z_ref, mod_ref, gn_ref, gpost_ref, wglu_ref, wout_ref, out_ref,
                    *, n_heads, head_dim, s5_w):
    s = _gelu(jnp.concatenate([y_ref[j].reshape(x_ref.shape[1], LANES) for j in range(y_ref.shape[0])], axis=1))
    s = s * _sigmoid(_dot(s.astype(BF16), wglu_ref[...]))
    o = o_ref[0]
    z = z_ref[0].astype(F32)
    parts = []
    for h in range(n_heads):
        sl = slice(h * head_dim, (h + 1) * head_dim)
        zh = z[:, sl]
        parts.append(_rms(o[:, sl], gn_ref[...]) * (zh * _sigmoid(zh)))
    gd = jnp.concatenate(parts, axis=-1)
    mixed = _dot(s.astype(BF16), wout_ref[0:s5_w, :]) + _dot(gd.astype(BF16), wout_ref[s5_w:, :])
    gate = mod_ref[0][2:3]
    out_ref[0] = x_ref[0] + gate * _rms(mixed, gpost_ref[...])


def _postmix(x, y, o, z, mod, gdn_norm, g_post, w_glu, w_out, n_heads, head_dim):
    b, l, d = x.shape
    s5_w = y.shape[0] * y.shape[-1]
    gw = o.shape[-1]
    tm = min(512, l)
    tc = S5_CHUNK
    tok = lambda i, t: (i, t, 0)
    const = lambda i, t: (0, 0)
    return pl.pallas_call(
        functools.partial(_postmix_kernel, n_heads=n_heads, head_dim=head_dim, s5_w=s5_w),
        grid=(b, l // tm),
        in_specs=[pl.BlockSpec((1, tm, d), tok), pl.BlockSpec((s5_w // LANES, tm // tc, None, tc, LANES), lambda i, t: (0, t, i, 0, 0)),
                  pl.BlockSpec((1, tm, gw), tok), pl.BlockSpec((1, tm, gw), tok),
                  pl.BlockSpec((1,) + mod.shape[1:], lambda i, t: (i, 0, 0)),
                  pl.BlockSpec((1, head_dim), const), pl.BlockSpec((1, d), const),
                  pl.BlockSpec((s5_w, s5_w), const), pl.BlockSpec((s5_w + gw, d), const)],
        out_specs=pl.BlockSpec((1, tm, d), tok),
        out_shape=jax.ShapeDtypeStruct((b, l, d), F32),
        compiler_params=_cparams(("parallel", "parallel")),
        name="postmix",
    )(x, y, o, z, mod, gdn_norm, g_post, w_glu, w_out)


def _peer_prep_kernel(x_ref, mod_ref, g_ref, wq_ref, keys_ref, ht_ref, r2_ref, e2_ref, n1_ref, e1_ref,
                      s1_ref, s2_ref, lv1_ref, lv2_ref, tau_ref, *, n_keys, topk, tt):
    @pl.when(pl.program_id(1) == 0)
    def _():
        m = mod_ref[0]
        h = _rms(x_ref[0], g_ref[...]) * (1.0 + m[4:5]) + m[3:4]
        ht_ref[...] = h.T.astype(BF16)

    half = keys_ref.shape[-1]
    ht = ht_ref[...]
    q1 = _dot(wq_ref[0:half, :], ht).astype(BF16)
    q2 = _dot(wq_ref[half:2 * half, :], ht).astype(BF16)
    s1_ref[...] = _dot(keys_ref[0, 0], q1)
    s2_ref[...] = _dot(keys_ref[0, 1], q2)
    kf = float(topk)

    nt = tt // LANES

    def lane_tile(t):
        return pl.ds(pl.multiple_of(t * LANES, LANES), LANES)

    def levels(t, _):
        cols = lane_tile(t)

        def lv_body(q, carry):
            c1, c2, rank2 = carry
            m1 = jnp.max(c1, axis=0, keepdims=True)
            m2 = jnp.max(c2, axis=0, keepdims=True)
            lv1_ref[t, pl.ds(q, 1), :] = m1
            lv2_ref[t, pl.ds(q, 1), :] = m2
            h1 = c1 == jnp.where(m1 > NEG, m1, -NEG)
            h2 = c2 == jnp.where(m2 > NEG, m2, -NEG)
            rank2 = jnp.where(h2, jnp.asarray(q, F32), rank2)
            return jnp.where(h1, NEG, c1), jnp.where(h2, NEG, c2), rank2

        s1, s2 = s1_ref[:, cols], s2_ref[:, cols]
        _, _, rank2 = lax.fori_loop(0, topk, lv_body, (s1, s2, jnp.full(s2.shape, kf, F32)))
        r2_ref[:, cols] = rank2.astype(BF16)
        return 0

    lax.fori_loop(0, nt, levels, 0)

    a_all, b_all = (jnp.concatenate([r[j] for j in range(nt)], axis=1) for r in (lv1_ref, lv2_ref))
    sub = topk // 2
    cand = jnp.concatenate([a_all[0:1] + b_all] + [a_all[p:p + 1] + b_all[0:sub] for p in range(1, sub)]
                           + [a_all[sub:] + b_all[0:1]], axis=0)

    def tau_body(_, carry):
        cand, cnt, tau = carry
        mx = jnp.max(cand, axis=0, keepdims=True)
        hit = cand == mx
        tau = jnp.where(cnt < kf, mx, tau)
        cnt = cnt + jnp.sum(jnp.where(hit, 1.0, 0.0), axis=0, keepdims=True)
        return jnp.where(hit, NEG, cand), cnt, tau

    row0 = a_all[0:1]
    _, _, tau_all = lax.fori_loop(0, topk, tau_body, (cand, jnp.zeros_like(row0), jnp.full_like(row0, NEG)))
    for j in range(nt):
        tau_ref[j] = jnp.broadcast_to(tau_all[:, j * LANES:(j + 1) * LANES], tau_ref.shape[1:])

    def gates(t, _):
        cols = lane_tile(t)
        s1, s2 = s1_ref[:, cols], s2_ref[:, cols]
        a_mat, b_mat, tau = lv1_ref[t], lv2_ref[t], tau_ref[t][0:1]
        a0, b0 = a_mat[0:1], b_mat[0:1]

        def n_body(q, carry):
            n1, c2 = carry
            b = lv2_ref[t, pl.ds(q, 1), :]
            n1 = n1 + jnp.where(s1 + b >= tau, 1.0, 0.0)
            c2 = c2 + jnp.where(a_mat + b >= tau, jnp.exp(b - b0), 0.0)
            return n1, c2

        n1, c2 = lax.fori_loop(0, topk, n_body, (jnp.zeros_like(s1), jnp.zeros_like(a_mat)))
        z = jnp.sum(jnp.exp(a_mat - a0) * c2, axis=0, keepdims=True)
        e2_ref[:, cols] = (jnp.exp(s2 - b0) / z).astype(BF16)
        n1_ref[0, :, cols] = n1
        e1_ref[0, :, cols] = jnp.exp(s1 - a0)
        return 0

    lax.fori_loop(0, nt, gates, 0)


def _peer_main_kernel(ht_ref, u_ref, vt_ref, r2_ref, e2_ref, n1_ref, e1_ref, x_ref, mod_ref, g_ref,
                      out_ref, acc_ref, hs0_ref, hs1_ref, act0_ref, act1_ref, r2s_ref, e2s_ref, fence_ref,
                      *, n_heads, n_keys, eb, tt, n_blocks):
    k = pl.program_id(1)
    nsub = eb // n_keys
    pk = BF16_ROWS
    grp = n_keys // pk

    @pl.when(k == 0)
    def _():
        acc_ref[...] = jnp.zeros_like(acc_ref)
        hs1_ref[...] = jnp.zeros_like(hs1_ref)
        act0_ref[...] = jnp.zeros_like(act0_ref)
        act1_ref[...] = jnp.zeros_like(act1_ref)
        r2s_ref[...] = r2_ref[...]
        e2s_ref[...] = e2_ref[...]

    def gate_tiles(hs_ref, act_ref, blk, ii):
        i = blk * nsub + ii
        nrows, erows = n1_ref[i], e1_ref[i]
        rs = slice(ii * n_keys, (ii + 1) * n_keys)
        for tc in range(tt // LANES):
            cs_ = slice(tc * LANES, (tc + 1) * LANES)
            gsum = jnp.zeros((n_keys, LANES), BF16)
            for hd in range(n_heads):
                nb, ebc = (jnp.concatenate([jnp.broadcast_to(r[hd:hd + 1, cs_], (pk, LANES)).astype(BF16)] * grp, 0)
                           for r in (nrows, erows))
                ks = slice(hd * n_keys, (hd + 1) * n_keys)
                r2, e2 = r2s_ref[ks, cs_], e2s_ref[ks, cs_]
                gsum = gsum + jnp.where(r2 < nb, e2, jnp.zeros_like(e2)) * ebc
            act_ref[rs, cs_] = _gelu(hs_ref[rs, cs_]).astype(BF16) * gsum

    d = ht_ref.shape[0]
    kc = d // nsub
    ec = eb // 2

    def half_step(u_rows, hs_w, hs_r, act_w, act_r, blk):
        for ii in range(nsub):
            u_part = pltpu.bitcast(u_ref[u_rows // 2:(u_rows + eb) // 2, ii * kc:(ii + 1) * kc], BF16)
            part = _dot(u_part, ht_ref[ii * kc:(ii + 1) * kc, :])
            if ii == 0:
                hs_w[...] = part
            else:
                hs_w[...] += part
            gate_tiles(hs_r, act_w, blk, ii)
            if ii % 2 == 0:
                j = ii // 2
                vt_part = pltpu.bitcast(vt_ref[:, u_rows + j * ec:u_rows + (j + 1) * ec], BF16)
                acc_ref[...] += _dot(vt_part, act_r[j * ec:(j + 1) * ec, :])

    half_step(0, hs0_ref, hs1_ref, act1_ref, act0_ref, jnp.maximum(2 * k - 1, 0))
    pl.semaphore_signal(fence_ref, 1)
    pl.semaphore_wait(fence_ref, 1)
    half_step(eb, hs1_ref, hs0_ref, act0_ref, act1_ref, jnp.minimum(2 * k, n_blocks - 1))

    @pl.when(k == pl.num_programs(1) - 1)
    def _():
        o = acc_ref[...].T
        out_ref[0] = x_ref[0] + mod_ref[0][5:6] * _rms(o, g_ref[...])


def _peer(x1, mod, g_pre, g_post, w_q, keys, exp_u, exp_v):
    b, l, d = x1.shape
    n_heads, _, n_keys, half = keys.shape
    n_exp = exp_u.shape[0]
    ntok = b * l
    tt = min(512, l)
    eb = 512
    tpb = l // tt
    wq_t = w_q.T.astype(BF16)
    keys_b = keys.astype(BF16)
    u_b = _pack_rows(exp_u, False)
    vt_b = _pack_rows(exp_v, True)
    tokb = lambda t, h: (t // tpb, t % tpb, 0)
    modb = lambda t, h: (t // tpb, 0, 0)
    hmap = lambda t, h: (h, 0, t)
    hshape = jax.ShapeDtypeStruct((n_heads, n_keys, ntok), F32)
    hshape_b = jax.ShapeDtypeStruct((n_heads * n_keys, ntok), BF16)
    ht, r2, e2, n1, e1 = pl.pallas_call(
        functools.partial(_peer_prep_kernel, n_keys=n_keys, topk=PEER_TOPK, tt=tt),
        grid=(ntok // tt, n_heads),
        in_specs=[pl.BlockSpec((1, tt, d), tokb), pl.BlockSpec((1,) + mod.shape[1:], modb),
                  pl.BlockSpec((1, d), lambda t, h: (0, 0)),
                  pl.BlockSpec((2 * half, d), lambda t, h: (h, 0)),
                  pl.BlockSpec((1, 2, n_keys, half), lambda t, h: (h, 0, 0, 0))],
        out_specs=[pl.BlockSpec((d, tt), lambda t, h: (0, t))] + [pl.BlockSpec((n_keys, tt), lambda t, h: (h, t))] * 2
                  + [pl.BlockSpec((1, n_keys, tt), hmap)] * 2,
        out_shape=[jax.ShapeDtypeStruct((d, ntok), BF16), hshape_b, hshape_b, hshape, hshape],
        scratch_shapes=[pltpu.VMEM((n_keys, tt), F32), pltpu.VMEM((n_keys, tt), F32),
                        pltpu.VMEM((tt // LANES, PEER_TOPK, LANES), F32), pltpu.VMEM((tt // LANES, PEER_TOPK, LANES), F32),
                        pltpu.VMEM((tt // LANES, 8, LANES), F32)],
        compiler_params=_cparams(("parallel", "arbitrary")),
        name="peer_prep",
    )(x1, mod, g_pre, wq_t, keys_b)

    n_blocks = n_exp // eb
    n_pairs = n_blocks // 2
    tokb2 = lambda t, k: (t // tpb, t % tpb, 0)
    hmap2 = lambda t, k: (0, 0, t)
    return pl.pallas_call(
        functools.partial(_peer_main_kernel, n_heads=n_heads, n_keys=n_keys, eb=eb, tt=tt, n_blocks=n_blocks),
        grid=(ntok // tt, n_pairs + 1),
        in_specs=[pl.BlockSpec((d, tt), lambda t, k: (0, t)),
                  pl.BlockSpec((eb, d), lambda t, k: (jnp.minimum(k, n_pairs - 1), 0)),
                  pl.BlockSpec((d // 2, 2 * eb), lambda t, k: (0, jnp.maximum(k - 1, 0)))]
                 + [pl.BlockSpec((n_heads * n_keys, tt), lambda t, k: (0, t))] * 2
                 + [pl.BlockSpec((n_keys, n_heads, tt), hmap2)] * 2
                 + [pl.BlockSpec((1, tt, d), tokb2),
                    pl.BlockSpec((1,) + mod.shape[1:], lambda t, k: (t // tpb, 0, 0)),
                    pl.BlockSpec((1, d), lambda t, k: (0, 0))],
        out_specs=pl.BlockSpec((1, tt, d), tokb2),
        out_shape=jax.ShapeDtypeStruct((b, l, d), F32),
        scratch_shapes=[pltpu.VMEM((d, tt), F32), pltpu.VMEM((eb, tt), F32), pltpu.VMEM((eb, tt), F32),
                        pltpu.VMEM((eb, tt), BF16), pltpu.VMEM((eb, tt), BF16),
                        pltpu.VMEM((n_heads * n_keys, tt), BF16), pltpu.VMEM((n_heads * n_keys, tt), BF16),
                        pltpu.SemaphoreType.REGULAR],
        compiler_params=_cparams(("parallel", "arbitrary")),
        name="peer_main",
    )(ht, u_b, vt_b, r2, e2, n1.transpose(1, 0, 2), e1.transpose(1, 0, 2), x1, mod, g_post)


def kernel(x, c, ctx, c_ctx, w_mod, b_mod, g_pre_mix, g_post_mix, g_pre_ffn, g_post_ffn, w_in, w_out,
           s5_lam_re, s5_lam_im, s5_log_dt, s5_b_re, s5_b_im, s5_c_re, s5_c_im, s5_d, s5_w_glu,
           gdn_conv, gdn_a_log, gdn_dt_bias, gdn_norm, peer_w_q, peer_keys, peer_u, peer_v):
    assert w_mod.shape[0] == 1, "single layer: the context stream only feeds the latent scans"
    b, l, d = x.shape
    lc = ctx.shape[1]
    n_heads = gdn_a_log.shape[-1]
    head_dim = gdn_norm.shape[-1]
    gdn_w = n_heads * head_dim
    s5_w = s5_d.shape[-1]
    n_gate = 2 * 2 * n_heads

    n_cond = -(-(b + 1) // 8) * 8
    cond = jnp.concatenate([c, c_ctx[None], jnp.zeros((n_cond - b - 1, d), F32)], axis=0)
    mod = _adaln(cond, w_mod[0], b_mod[0]).reshape(n_cond, 6, d)

    n_proj = w_in.shape[-1]
    w_pad = jnp.pad(w_in[0], ((0, 0), (0, -n_proj % LANES))).astype(BF16)
    g1 = g_pre_mix[0][None]
    ux, qkvx, zx, abx = _premix(x, mod[:b], True, g1, w_pad, s5_w, 3 * gdn_w, gdn_w)
    uc, qkvc, _, abc = _premix(ctx, mod[b:b + 1], False, g1, w_pad, s5_w, 3 * gdn_w, gdn_w)
    assert abx.shape[-1] == LANES and n_gate <= LANES

    mats = _s5_matrices(s5_lam_re[0], s5_lam_im[0], s5_log_dt[0], s5_b_re[0], s5_b_im[0],
                        s5_c_re[0], s5_c_im[0], s5_d[0])
    y_s5 = _s5(ux, uc, mats)

    qkvx = _gdn_conv(qkvx, gdn_conv[0], GRID_W, l // GRID_W > 1, n_heads, head_dim)
    qkvc = _gdn_conv(qkvc, gdn_conv[0], lc, False, n_heads, head_dim)
    o_gdn = _gdn(qkvx, abx, qkvc, abc, gdn_a_log[0], gdn_dt_bias[0], n_heads, head_dim)

    x1 = _postmix(x, y_s5, o_gdn, zx, mod[:b], gdn_norm[0][None], g_post_mix[0][None],
                  s5_w_glu[0].astype(BF16), w_out[0].astype(BF16), n_heads, head_dim)
    return _peer(x1, mod[:b], g_pre_ffn[0][None], g_post_ffn[0][None],
                 peer_w_q[0], peer_keys[0], peer_u[0], peer_v[0])
```

```python
import functools

import jax
import jax.numpy as jnp
from jax import lax
from jax.experimental import pallas as pl
from jax.experimental.pallas import tpu as pltpu

F32 = jnp.float32
BF16 = jnp.bfloat16
HI = lax.Precision.HIGHEST

EPS = 1e-6
GRID_W = 64
GDN_CHUNK = 64
S5_CHUNK = 8
PEER_TOPK = 16
LANES = 128
BF16_ROWS = 16
NEG = -1e30
VMEM_LIMIT = 56 * 1024 * 1024


def _cparams(sem):
    return pltpu.CompilerParams(dimension_semantics=sem, vmem_limit_bytes=VMEM_LIMIT)


def _sigmoid(x):
    return 1.0 / (1.0 + jnp.exp(-x))


def _softplus(x):
    return jnp.maximum(x, 0.0) + jnp.log(1.0 + jnp.exp(-jnp.abs(x)))


def _gelu(x):
    return 0.5 * x * (1.0 + lax.erf(x * (2.0 ** -0.5)))


def _rms(x, g):
    return x * lax.rsqrt(jnp.mean(x * x, axis=-1, keepdims=True) + EPS) * g


def _dot(a, b, dims=None, precision=None):
    if dims is None:
        dims = (((a.ndim - 1,), (0,)), ((), ()))
    return lax.dot_general(a, b, dims, precision=precision, preferred_element_type=F32)


def _pack_kernel(x_ref, o_ref, *, transpose):
    x = x_ref[...]
    x = x.T if transpose else x
    o_ref[...] = pltpu.bitcast(x.astype(BF16), jnp.uint32)


def _pack_rows(x, transpose):
    n, m = x.shape
    tn = 512
    if transpose:
        out_shape, out_spec = (m // 2, n), pl.BlockSpec((m // 2, tn), lambda i: (0, i))
    else:
        out_shape, out_spec = (n // 2, m), pl.BlockSpec((tn // 2, m), lambda i: (i, 0))
    return pl.pallas_call(
        functools.partial(_pack_kernel, transpose=transpose),
        grid=(n // tn,),
        in_specs=[pl.BlockSpec((tn, m), lambda i: (i, 0))],
        out_specs=out_spec,
        out_shape=jax.ShapeDtypeStruct(out_shape, jnp.uint32),
        compiler_params=_cparams(("parallel",)),
        name="pack_table",
    )(x)


_NT = (((1,), (1,)), ((), ()))
_TN = (((0,), (0,)), ((), ()))


def _adaln_kernel(c_ref, w_ref, b_ref, o_ref):
    c = c_ref[...]
    o_ref[...] = _dot(c * _sigmoid(c), w_ref[...], precision=HI) + b_ref[...]


def _adaln(cond, w_mod, b_mod):
    n, d = cond.shape
    m = w_mod.shape[1]
    tn = 512
    return pl.pallas_call(
        _adaln_kernel,
        grid=(m // tn,),
        in_specs=[pl.BlockSpec((n, d), lambda j: (0, 0)),
                  pl.BlockSpec((d, tn), lambda j: (0, j)),
                  pl.BlockSpec((1, tn), lambda j: (0, j))],
        out_specs=pl.BlockSpec((n, tn), lambda j: (0, j)),
        out_shape=jax.ShapeDtypeStruct((n, m), F32),
        compiler_params=_cparams(("arbitrary",)),
        name="adaln",
    )(cond, w_mod, b_mod.reshape(1, m))


def _premix_kernel(x_ref, mod_ref, g_ref, w_ref, u_ref, qkv_ref, z_ref, ab_ref, *, n_u, n_qkv, n_z):
    m = mod_ref[0]
    h = _rms(x_ref[0], g_ref[...]) * (1.0 + m[1:2]) + m[0:1]
    p = _dot(h.astype(BF16), w_ref[...])
    for j in range(u_ref.shape[0]):
        u_ref[j] = p[:, j * LANES:(j + 1) * LANES].reshape(u_ref.shape[1:])
    qkv_ref[0] = p[:, n_u:n_u + n_qkv]
    z_ref[0] = p[:, n_u + n_qkv:n_u + n_qkv + n_z].astype(BF16)
    ab_ref[0] = p[:, n_u + n_qkv + n_z:]


def _premix(x, mod, per_batch, g, w_pad, n_u, n_qkv, n_z):
    b, l, d = x.shape
    tm = min(512, l)
    tc = S5_CHUNK
    n_all = w_pad.shape[1]
    n_ab = n_all - n_u - n_qkv - n_z
    mod_map = (lambda i, t: (i, 0, 0)) if per_batch else (lambda i, t: (0, 0, 0))
    tok = lambda i, t: (i, t, 0)
    return pl.pallas_call(
        functools.partial(_premix_kernel, n_u=n_u, n_qkv=n_qkv, n_z=n_z),
        grid=(b, l // tm),
        in_specs=[pl.BlockSpec((1, tm, d), tok),
                  pl.BlockSpec((1,) + mod.shape[1:], mod_map),
                  pl.BlockSpec((1, d), lambda i, t: (0, 0)),
                  pl.BlockSpec((d, n_all), lambda i, t: (0, 0))],
        out_specs=[pl.BlockSpec((n_u // LANES, tm // tc, None, tc, LANES), lambda i, t: (0, t, i, 0, 0)),
                   pl.BlockSpec((1, tm, n_qkv), tok),
                   pl.BlockSpec((1, tm, n_z), tok), pl.BlockSpec((1, tm, n_ab), tok)],
        out_shape=[jax.ShapeDtypeStruct((n_u // LANES, l // tc, b, tc, LANES), F32), jax.ShapeDtypeStruct((b, l, n_qkv), F32),
                   jax.ShapeDtypeStruct((b, l, n_z), BF16), jax.ShapeDtypeStruct((b, l, n_ab), F32)],
        compiler_params=_cparams(("parallel", "parallel")),
        name="premix",
    )(x, mod, g, w_pad)


def _s5_matrices(lam_re, lam_im, log_dt, b_re, b_im, c_re, c_im, d_skip):
    t = S5_CHUNK
    n_dirs, g, p = lam_re.shape
    cg = b_re.shape[-1]
    dt = jnp.exp(log_dt)[..., None]
    lr, li = lam_re, lam_im
    den = lr * lr + li * li
    mag = jnp.exp(lr * dt)
    lb_re, lb_im = mag * jnp.cos(li * dt), mag * jnp.sin(li * dt)
    nr = lb_re - 1.0
    cr = (nr * lr + lb_im * li) / den
    ci = (lb_im * lr - nr * li) / den
    bb_re = cr[..., None] * b_re - ci[..., None] * b_im
    bb_im = cr[..., None] * b_im + ci[..., None] * b_re
    taus = jnp.arange(t + 1, dtype=F32)[:, None, None, None]
    pmag = jnp.exp(taus * (lr * dt))
    pw_re = pmag * jnp.cos(taus * (li * dt))
    pw_im = pmag * jnp.sin(taus * (li * dt))
    ein = functools.partial(jnp.einsum, precision=HI)
    lbr = pw_re[..., None] * bb_re - pw_im[..., None] * bb_im
    lbi = pw_re[..., None] * bb_im + pw_im[..., None] * bb_re
    kk = ein('dgcp,tdgpe->tdgce', c_re, lbr) - ein('dgcp,tdgpe->tdgce', c_im, lbi)
    er = c_re[None] * pw_re[:, :, :, None, :] - c_im[None] * pw_im[:, :, :, None, :]
    ei = c_re[None] * pw_im[:, :, :, None, :] + c_im[None] * pw_re[:, :, :, None, :]

    s_idx = jnp.arange(t)
    lag = s_idx[None, :] - s_idx[:, None]
    kf = kk[:, 0][jnp.clip(lag, 0, t)]
    kb = kk[:, 1][jnp.clip(-lag, 0, t)]
    m_full = (jnp.where((lag >= 0)[:, :, None, None, None], kf, 0.0)
              + jnp.where((lag <= 0)[:, :, None, None, None], kb, 0.0))
    eye_t = jnp.eye(t, dtype=F32)[:, :, None, None, None]
    eye_c = jnp.eye(cg, dtype=F32)[None, None, None]
    m_full = m_full + eye_t * eye_c * d_skip.reshape(g, cg)[None, None, :, :, None]
    p_parts = (lbr[:t, 0][::-1], lbi[:t, 0][::-1], lbr[:t, 1], lbi[:t, 1])
    r_parts = (er[1:, 0], -ei[1:, 0], er[1:, 1][::-1], -ei[1:, 1][::-1])

    gw = LANES // cg
    n8 = g // gw
    eye = jnp.eye(gw, dtype=F32)
    m6 = m_full.reshape(t, t, n8, gw, cg, cg).transpose(2, 0, 3, 5, 1, 4)
    mm = (m6[:, :, :, :, :, None, :] * eye[None, None, :, None, None, :, None]).reshape(n8, t * LANES, t * LANES)

    def p_full(a):
        a6 = a.reshape(t, n8, gw, p, cg).transpose(1, 0, 2, 4, 3)
        return a6[:, :, :, :, None, :] * eye[None, None, :, None, :, None]

    pq = jnp.stack([p_full(x) for x in p_parts], axis=4).reshape(n8, t * LANES, 4 * gw * p)

    def r_full(a):
        a6 = a.reshape(t, n8, gw, cg, p).transpose(1, 2, 4, 0, 3)
        return a6[:, :, :, :, None, :] * eye[None, :, None, None, :, None]

    rr = jnp.stack([r_full(x) for x in r_parts], axis=1).reshape(n8, 4 * gw * p, t * LANES)
    dec = jnp.stack([x.reshape(n8, gw * p) for x in (pw_re[t, 0], pw_im[t, 0], pw_re[t, 1], pw_im[t, 1])], axis=1)
    dec = jnp.concatenate([dec, jnp.zeros_like(dec)], axis=1)
    return pq.astype(BF16), mm.astype(BF16), rr.astype(BF16), dec


def _chunk_rows(ref, nb, t):
    n = ref.shape[0]
    return jnp.concatenate([ref[:, :, s, :].reshape(n * nb, LANES) for s in range(t)], axis=1).astype(BF16)


def _s5_state_kernel(x_ref, c_ref, pq_ref, dec_ref, sin_ref, loc_ref, locc_ref, *, nb, t, nct, ncx, ncc, sw):
    rt = pl.program_id(1)
    rows = nct * nb
    loc_ref[pl.ds(pl.multiple_of(rt * rows, rows), rows), :] = _dot(_chunk_rows(x_ref, nb, t), pq_ref[0])

    @pl.when(rt == 0)
    def _():
        locc_ref[...] = _dot(_chunk_rows(c_ref, nb, t), pq_ref[0])

    @pl.when(rt == pl.num_programs(1) - 1)
    def _():
        dec = dec_ref[0]
        far, fai, bar, bai = (jnp.broadcast_to(dec[i:i + 1], (nb, sw)) for i in range(4))

        def step(state, ar, ai, ref, k, c0):
            re, im = state
            r = pl.ds(pl.multiple_of(k * nb, nb), nb)
            return ar * re - ai * im + ref[r, c0:c0 + sw], ar * im + ai * re + ref[r, c0 + sw:c0 + 2 * sw]

        zero = jnp.zeros((nb, sw), F32)

        def ctx_body(i, carry):
            f, bk = carry
            return step(f, far, fai, locc_ref, i, 0), step(bk, bar, bai, locc_ref, ncc - 1 - i, 2 * sw)

        carry = lax.fori_loop(0, ncc, ctx_body, ((zero, zero), (zero, zero)))

        def pair_store(k0, c0, s_lo, s_hi):
            r = pl.ds(pl.multiple_of(k0 * nb, 2 * nb), 2 * nb)
            sin_ref[0, r, c0:c0 + sw] = jnp.concatenate([s_lo[0], s_hi[0]], axis=0).astype(BF16)
            sin_ref[0, r, c0 + sw:c0 + 2 * sw] = jnp.concatenate([s_lo[1], s_hi[1]], axis=0).astype(BF16)

        def x_body(j, carry):
            f0, b0 = carry
            f1 = step(f0, far, fai, loc_ref, 2 * j, 0)
            pair_store(2 * j, 0, f0, f1)
            kb = ncx - 2 - 2 * j
            b1 = step(b0, bar, bai, loc_ref, kb + 1, 2 * sw)
            pair_store(kb, 2 * sw, b1, b0)
            return step(f1, far, fai, loc_ref, 2 * j + 1, 0), step(b1, bar, bai, loc_ref, kb, 2 * sw)

        lax.fori_loop(0, ncx // 2, x_body, carry)


def _s5_out_kernel(x_ref, sin_ref, mm_ref, rr_ref, y_ref, *, nb, t):
    n = x_ref.shape[0]
    y = _dot(_chunk_rows(x_ref, nb, t), mm_ref[0]) + _dot(sin_ref[0], rr_ref[0])
    for s in range(t):
        y_ref[:, :, s, :] = y[:, s * LANES:(s + 1) * LANES].reshape(n, nb, LANES)


def _s5(ux, uc, mats):
    pq, mm, rr, dec = mats
    n8, ncx, nb, t, _ = ux.shape
    ncc = uc.shape[1]
    sw = dec.shape[-1]
    assert nb % 8 == 0 and t == S5_CHUNK and ncx % 2 == 0 and n8 == pq.shape[0]
    nct = min(64, ncx)
    ux5, uc5 = ux, uc
    kdim = t * LANES
    xblk = pl.BlockSpec((None, nct, nb, t, LANES), lambda g, r: (g, r, 0, 0, 0))
    wmap = lambda g, r: (g, 0, 0)
    sin = pl.pallas_call(
        functools.partial(_s5_state_kernel, nb=nb, t=t, nct=nct, ncx=ncx, ncc=ncc, sw=sw),
        grid=(n8, ncx // nct),
        in_specs=[xblk, pl.BlockSpec((None, ncc, nb, t, LANES), lambda g, r: (g, 0, 0, 0, 0)),
                  pl.BlockSpec((1, kdim, 4 * sw), wmap), pl.BlockSpec((1, 8, sw), wmap)],
        out_specs=pl.BlockSpec((1, ncx * nb, 4 * sw), wmap),
        out_shape=jax.ShapeDtypeStruct((n8, ncx * nb, 4 * sw), BF16),
        scratch_shapes=[pltpu.VMEM((ncx * nb, 4 * sw), F32), pltpu.VMEM((ncc * nb, 4 * sw), F32)],
        compiler_params=_cparams(("parallel", "arbitrary")),
        name="s5_state",
    )(ux5, uc5, pq, dec)
    y = pl.pallas_call(
        functools.partial(_s5_out_kernel, nb=nb, t=t),
        grid=(n8, ncx // nct),
        in_specs=[xblk, pl.BlockSpec((1, nct * nb, 4 * sw), lambda g, r: (g, r, 0)),
                  pl.BlockSpec((1, kdim, kdim), wmap), pl.BlockSpec((1, 4 * sw, kdim), wmap)],
        out_specs=xblk,
        out_shape=jax.ShapeDtypeStruct((n8, ncx, nb, t, LANES), F32),
        compiler_params=_cparams(("parallel", "parallel")),
        name="s5_out",
    )(ux5, sin, mm, rr)
    return y


_CONV_PAD = 72


def _gdn_conv_kernel(x_ref, w_ref, o_ref, pad_ref, *, length, cols, two_d, n_norm, n_q, q_scale):
    cb = pl.program_id(1)
    pad = _CONV_PAD
    zeros = jnp.zeros((pad, LANES), F32)
    pad_ref[0:pad, :] = zeros
    pad_ref[pad + length:pad + length + pad, :] = zeros
    pad_ref[pad:pad + length, :] = x_ref[0]
    rc = min(256, length)
    for r0 in range(0, length, rc):
        col = lax.rem(lax.broadcasted_iota(jnp.int32, (rc, LANES), 0) + r0, cols)
        acc = jnp.zeros((rc, LANES), F32)
        for dr in ((0, 1, 2) if two_d else (1,)):
            for dc in range(3):
                off = (dr - 1) * cols + (dc - 1)
                term = pad_ref[pad + r0 + off:pad + r0 + off + rc, :] * w_ref[dr * 3 + dc:dr * 3 + dc + 1, :]
                if dc == 0:
                    term = jnp.where(col >= 1, term, 0.0)
                if dc == 2:
                    term = jnp.where(col <= cols - 2, term, 0.0)
                acc = acc + term
        y = acc * _sigmoid(acc)
        nrm = y * lax.rsqrt(jnp.sum(y * y, axis=-1, keepdims=True) + EPS)
        nrm = nrm * jnp.where(cb < n_q, q_scale, 1.0)
        o_ref[0, r0:r0 + rc, :] = jnp.where(cb < n_norm, nrm, y)


def _gdn_conv(qkv, conv_w, cols, two_d, n_heads, head_dim):
    b, l, ch = qkv.shape
    w9 = jnp.concatenate([conv_w.reshape(9, ch), jnp.zeros((7, ch), F32)], axis=0)
    blk = lambda i, j: (i, 0, j)
    return pl.pallas_call(
        functools.partial(_gdn_conv_kernel, length=l, cols=cols, two_d=two_d,
                          n_norm=2 * n_heads, n_q=n_heads, q_scale=head_dim ** -0.5),
        grid=(b, ch // LANES),
        in_specs=[pl.BlockSpec((1, l, LANES), blk), pl.BlockSpec((16, LANES), lambda i, j: (0, j))],
        out_specs=pl.BlockSpec((1, l, LANES), blk),
        out_shape=jax.ShapeDtypeStruct((b, l, ch), F32),
        scratch_shapes=[pltpu.VMEM((l + 2 * _CONV_PAD, LANES), F32)],
        compiler_params=_cparams(("parallel", "parallel")),
        name="gdn_conv",
    )(qkv, w9)


def _gdn_kernel(qx_ref, kx_ref, vx_ref, abx_ref, qc_ref, kc_ref, vc_ref, abc_ref,
                pl_ref, o_ref, gcx_ref, bcx_ref, gcc_ref, bcc_ref,
                u_ref, w_ref, qd_ref, kd_ref, at_ref, gt_ref, *, n_heads, ncx, ncc):
    cs = GDN_CHUNK
    hd = qx_ref.shape[-1]
    head = pl.program_id(1)
    alog_l, dtb_l = pl_ref[0:1, :], pl_ref[1:2, :]
    lane = lax.broadcasted_iota(jnp.int32, (1, LANES), 1)

    def gate_columns(ab_ref, g_out, b_out, n_rows):
        rc = min(256, n_rows)
        for r0 in range(0, n_rows, rc):
            ab = ab_ref[0, r0:r0 + rc, :]
            g_all = -jnp.exp(alog_l) * _softplus(ab + dtb_l)
            b_all = _sigmoid(ab)
            for dr in range(2):
                j = dr * n_heads + head
                g = jnp.sum(jnp.where(lane == j, g_all, 0.0), axis=-1, keepdims=True)
                bt = jnp.sum(jnp.where(lane == j + 2 * n_heads, b_all, 0.0), axis=-1, keepdims=True)
                g_out[dr, r0:r0 + rc, :] = jnp.broadcast_to(g, (rc, hd))
                b_out[dr, r0:r0 + rc, :] = jnp.broadcast_to(bt, (rc, hd))

    gate_columns(abx_ref, gcx_ref, bcx_ref, ncx * cs)
    gate_columns(abc_ref, gcc_ref, bcc_ref, ncc * cs)

    nblk = 4
    side = nblk * cs
    def block_masks():
        ri = lax.broadcasted_iota(jnp.int32, (side, side), 0)
        ci = lax.broadcasted_iota(jnp.int32, (side, side), 1)
        dist = jnp.where(ri < side // 2, ri - ci, ci - ri)
        dist = jnp.where((ri // cs) == (ci // cs), dist, -1)
        return dist >= 0, dist > 0, jnp.where(ri == ci, 1.0, 0.0)

    def cumsum_blocks(x):
        row_id = lax.broadcasted_iota(jnp.int32, (side, hd), 0)
        pos = row_id % cs
        is_fwd = row_id < side // 2
        s = 1
        while s < cs:
            x = x + jnp.where(is_fwd, jnp.where(pos >= s, pltpu.roll(x, s, 0), 0.0),
                              jnp.where(pos < cs - s, pltpu.roll(x, side - s, 0), 0.0))
            s *= 2
        return x

    def each(f, *lists):
        return [f(*args) for args in zip(*lists)]

    def chunks_local(c0s, refs, base):
        q_ref, k_ref, v_ref, gcol_ref, bcol_ref = refs
        incl, strict, eye = block_masks()
        rows = [pl.ds(pl.multiple_of(c0 * cs, 2 * cs), 2 * cs) for c0 in c0s]
        q, k, v = ([jnp.concatenate([r[0, rw, :]] * 2, axis=0) for rw in rows] for r in (q_ref, k_ref, v_ref))
        gb, bb = ([jnp.concatenate([r[0, rw, :], r[1, rw, :]], axis=0) for rw in rows] for r in (gcol_ref, bcol_ref))
        gcs = each(cumsum_blocks, gb)
        gwide = each(lambda g: jnp.concatenate([g] * (side // hd), axis=1), gcs)
        decay = each(lambda g: jnp.where(incl, jnp.exp(jnp.where(incl, g - g.T[0:1, :], 0.0)), 0.0), gwide)
        kbf = each(lambda t: t.astype(BF16), k)
        kb = each(lambda t, b_: t * b_, k, bb)
        lmat = each(lambda x, y, dc: jnp.where(strict, _dot(x.astype(BF16), y, _NT) * dc, 0.0), kb, kbf, decay)
        pw = each(lambda l_: -l_, lmat)
        tmat = each(lambda p_: eye + p_, pw)
        for _ in range(cs.bit_length() - 2):
            pw = each(lambda p_: _dot(p_.astype(BF16), p_.astype(BF16)), pw)
            tmat = each(lambda t_, p_: t_ + _dot(t_.astype(BF16), p_.astype(BF16)), tmat, pw)
        g_exp = each(jnp.exp, gcs)
        rhs = each(lambda v_, b_, kb_, ge: jnp.concatenate([v_ * b_, kb_ * ge], axis=1).astype(BF16), v, bb, kb, g_exp)
        uw = each(lambda t_, r_: _dot(t_.astype(BF16), r_), tmat, rhs)
        attn = each(lambda q_, y, dc: jnp.where(incl, _dot(q_.astype(BF16), y, _NT) * dc, 0.0), q, kbf, decay)
        half = side // 2
        for c0, uw_, attn_, q_, k_, g_, ge in zip(c0s, uw, attn, q, k, gcs, g_exp):
            lasts = [g_[(j + 1) * cs - 1:(j + 1) * cs, :] if j < nblk // 2 else g_[j * cs:j * cs + 1, :] for j in range(nblk)]
            g_last = jnp.concatenate([jnp.broadcast_to(r, (cs, hd)) for r in lasts], axis=0)
            qd = (q_ * ge).astype(BF16)
            kd = (k_ * jnp.exp(g_last - g_)).astype(BF16)
            srows = pl.ds(pl.multiple_of((base + c0) * cs, 2 * cs), 2 * cs)
            grows = pl.ds(pl.multiple_of((base + c0) * 8, 16), 16)
            for dr in range(2):
                blk = slice(dr * half, (dr + 1) * half)
                u_ref[dr, srows, :] = uw_[blk, :hd]
                w_ref[dr, srows, :] = uw_[blk, hd:].astype(BF16)
                qd_ref[dr, srows, :] = qd[blk]
                kd_ref[dr, srows, :] = kd[blk]
                at_ref[dr, srows, :] = jnp.concatenate(
                    [attn_[j * cs:(j + 1) * cs, j * cs:(j + 1) * cs] for j in (2 * dr, 2 * dr + 1)], axis=0).astype(BF16)
                gt_ref[dr, grows, :] = jnp.concatenate(
                    [jnp.broadcast_to(jnp.exp(lasts[j]), (8, hd)) for j in (2 * dr, 2 * dr + 1)], axis=0)

    def chunks_seq(cgs, states):
        drs = (0, 1)
        srows = [pl.ds(pl.multiple_of(cg * cs, cs), cs) for cg in cgs]
        sb = [st.astype(BF16) for st in states]
        vb = [(u_ref[dr, r, :] - _dot(w_ref[dr, r, :], s_)).astype(BF16) for dr, r, s_ in zip(drs, srows, sb)]
        new = [st * gt_ref[dr, pl.ds(pl.multiple_of(cg * 8, 8), 8), :][0:1] + _dot(kd_ref[dr, r, :], v_, _TN)
               for dr, cg, r, st, v_ in zip(drs, cgs, srows, states, vb)]
        outs = [_dot(qd_ref[dr, r, :], s_) + _dot(at_ref[dr, r, :], v_) for dr, r, s_, v_ in zip(drs, srows, sb, vb)]
        return outs, new

    refs_c = (qc_ref, kc_ref, vc_ref, gcc_ref, bcc_ref)
    refs_x = (qx_ref, kx_ref, vx_ref, gcx_ref, bcx_ref)

    def local_pass(n, refs, base):
        def body(i, _):
            chunks_local([4 * i, 4 * i + 2], refs, base)
            return 0

        lax.fori_loop(0, n // 4, body, 0)

    local_pass(ncc, refs_c, 0)
    local_pass(ncx, refs_x, ncc)

    zero = jnp.zeros((hd, hd), F32)

    def ctx_body(i, carry):
        return tuple(chunks_seq((i, ncc - 1 - i), carry)[1])

    carry = lax.fori_loop(0, ncc, ctx_body, (zero, zero))
    o_ref[...] = jnp.zeros_like(o_ref)

    def x_body(i, carry):
        ib = ncx - 1 - i
        (of, ob), new = chunks_seq((ncc + i, ncc + ib), carry)
        o_ref[0, pl.ds(pl.multiple_of(i * cs, cs), cs), :] += of
        o_ref[0, pl.ds(pl.multiple_of(ib * cs, cs), cs), :] += ob
        return tuple(new)

    lax.fori_loop(0, ncx, x_body, carry)


def _gdn(qkvx, abx, qkvc, abc, a_log, dt_bias, n_heads, head_dim):
    b, l, _ = qkvx.shape
    lc = qkvc.shape[1]
    cs = GDN_CHUNK
    ncx, ncc = l // cs, lc // cs
    assert ncx % 4 == 0 and ncc % 4 == 0, "chunks are processed four at a time"
    ng = 2 * n_heads
    p_l = jnp.zeros((8, LANES), F32)
    p_l = p_l.at[0, :ng].set(a_log.reshape(ng)).at[1, :ng].set(dt_bias.reshape(ng))

    def col(off):
        return lambda i, h: (i, 0, off + h)

    full = lambda i, h: (i, 0, 0)
    const = lambda i, h: (0, 0)
    hb = head_dim
    return pl.pallas_call(
        functools.partial(_gdn_kernel, n_heads=n_heads, ncx=ncx, ncc=ncc),
        grid=(b, n_heads),
        in_specs=[pl.BlockSpec((1, l, hb), col(0)), pl.BlockSpec((1, l, hb), col(n_heads)),
                  pl.BlockSpec((1, l, hb), col(2 * n_heads)),
                  pl.BlockSpec((1, l, LANES), full),
                  pl.BlockSpec((1, lc, hb), col(0)), pl.BlockSpec((1, lc, hb), col(n_heads)),
                  pl.BlockSpec((1, lc, hb), col(2 * n_heads)),
                  pl.BlockSpec((1, lc, LANES), full),
                  pl.BlockSpec((8, LANES), const)],
        out_specs=pl.BlockSpec((1, l, hb), col(0)),
        out_shape=jax.ShapeDtypeStruct((b, l, n_heads * head_dim), F32),
        scratch_shapes=[pltpu.VMEM((2, l, hb), F32), pltpu.VMEM((2, l, hb), F32),
                        pltpu.VMEM((2, lc, hb), F32), pltpu.VMEM((2, lc, hb), F32),
                        pltpu.VMEM((2, l + lc, hb), F32), pltpu.VMEM((2, l + lc, hb), BF16),
                        pltpu.VMEM((2, l + lc, hb), BF16), pltpu.VMEM((2, l + lc, hb), BF16),
                        pltpu.VMEM((2, l + lc, cs), BF16), pltpu.VMEM((2, (ncx + ncc) * 8, hb), F32)],
        compiler_params=_cparams(("parallel", "arbitrary")),
        name="gdn",
    )(qkvx, qkvx, qkvx, abx, qkvc, qkvc, qkvc, abc, p_l)


def _postmix_kernel(x_ref, y_ref, o_ref, z_ref, mod_ref, gn_ref, gpost_ref, wglu_ref, wout_ref, out_ref,
                    *, n_heads, head_dim, s5_w):
    s = _gelu(jnp.concatenate([y_ref[j].reshape(x_ref.shape[1], LANES) for j in range(y_ref.shape[0])], axis=1))
    s = s * _sigmoid(_dot(s.astype(BF16), wglu_ref[...]))
    o = o_ref[0]
    z = z_ref[0].astype(F32)
    parts = []
    for h in range(n_heads):
        sl = slice(h * head_dim, (h + 1) * head_dim)
        zh = z[:, sl]
        parts.append(_rms(o[:, sl], gn_ref[...]) * (zh * _sigmoid(zh)))
    gd = jnp.concatenate(parts, axis=-1)
    mixed = _dot(s.astype(BF16), wout_ref[0:s5_w, :]) + _dot(gd.astype(BF16), wout_ref[s5_w:, :])
    gate = mod_ref[0][2:3]
    out_ref[0] = x_ref[0] + gate * _rms(mixed, gpost_ref[...])


def _postmix(x, y, o, z, mod, gdn_norm, g_post, w_glu, w_out, n_heads, head_dim):
    b, l, d = x.shape
    s5_w = y.shape[0] * y.shape[-1]
    gw = o.shape[-1]
    tm = min(512, l)
    tc = S5_CHUNK
    tok = lambda i, t: (i, t, 0)
    const = lambda i, t: (0, 0)
    return pl.pallas_call(
        functools.partial(_postmix_kernel, n_heads=n_heads, head_dim=head_dim, s5_w=s5_w),
        grid=(b, l // tm),
        in_specs=[pl.BlockSpec((1, tm, d), tok), pl.BlockSpec((s5_w // LANES, tm // tc, None, tc, LANES), lambda i, t: (0, t, i, 0, 0)),
                  pl.BlockSpec((1, tm, gw), tok), pl.BlockSpec((1, tm, gw), tok),
                  pl.BlockSpec((1,) + mod.shape[1:], lambda i, t: (i, 0, 0)),
                  pl.BlockSpec((1, head_dim), const), pl.BlockSpec((1, d), const),
                  pl.BlockSpec((s5_w, s5_w), const), pl.BlockSpec((s5_w + gw, d), const)],
        out_specs=pl.BlockSpec((1, tm, d), tok),
        out_shape=jax.ShapeDtypeStruct((b, l, d), F32),
        compiler_params=_cparams(("parallel", "parallel")),
        name="postmix",
    )(x, y, o, z, mod, gdn_norm, g_post, w_glu, w_out)


def _peer_prep_kernel(x_ref, mod_ref, g_ref, wq_ref, keys_ref, ht_ref, r2_ref, e2_ref, n1_ref, e1_ref,
                      s1_ref, s2_ref, lv1_ref, lv2_ref, tau_ref, *, n_keys, topk, tt):
    @pl.when(pl.program_id(1) == 0)
    def _():
        m = mod_ref[0]
        h = _rms(x_ref[0], g_ref[...]) * (1.0 + m[4:5]) + m[3:4]
        ht_ref[...] = h.T.astype(BF16)

    half = keys_ref.shape[-1]
    ht = ht_ref[...]
    q1 = _dot(wq_ref[0:half, :], ht).astype(BF16)
    q2 = _dot(wq_ref[half:2 * half, :], ht).astype(BF16)
    s1_ref[...] = _dot(keys_ref[0, 0], q1)
    s2_ref[...] = _dot(keys_ref[0, 1], q2)
    kf = float(topk)

    nt = tt // LANES

    def lane_tile(t):
        return pl.ds(pl.multiple_of(t * LANES, LANES), LANES)

    def levels(t, _):
        cols = lane_tile(t)

        def lv_body(q, carry):
            c1, c2, rank2 = carry
            m1 = jnp.max(c1, axis=0, keepdims=True)
            m2 = jnp.max(c2, axis=0, keepdims=True)
            lv1_ref[t, pl.ds(q, 1), :] = m1
            lv2_ref[t, pl.ds(q, 1), :] = m2
            h1 = c1 == jnp.where(m1 > NEG, m1, -NEG)
            h2 = c2 == jnp.where(m2 > NEG, m2, -NEG)
            rank2 = jnp.where(h2, jnp.asarray(q, F32), rank2)
            return jnp.where(h1, NEG, c1), jnp.where(h2, NEG, c2), rank2

        s1, s2 = s1_ref[:, cols], s2_ref[:, cols]
        _, _, rank2 = lax.fori_loop(0, topk, lv_body, (s1, s2, jnp.full(s2.shape, kf, F32)))
        r2_ref[:, cols] = rank2.astype(BF16)
        return 0

    lax.fori_loop(0, nt, levels, 0)

    a_all, b_all = (jnp.concatenate([r[j] for j in range(nt)], axis=1) for r in (lv1_ref, lv2_ref))
    sub = topk // 2
    cand = jnp.concatenate([a_all[0:1] + b_all] + [a_all[p:p + 1] + b_all[0:sub] for p in range(1, sub)]
                           + [a_all[sub:] + b_all[0:1]], axis=0)

    def tau_body(_, carry):
        cand, cnt, tau = carry
        mx = jnp.max(cand, axis=0, keepdims=True)
        hit = cand == mx
        tau = jnp.where(cnt < kf, mx, tau)
        cnt = cnt + jnp.sum(jnp.where(hit, 1.0, 0.0), axis=0, keepdims=True)
        return jnp.where(hit, NEG, cand), cnt, tau

    row0 = a_all[0:1]
    _, _, tau_all = lax.fori_loop(0, topk, tau_body, (cand, jnp.zeros_like(row0), jnp.full_like(row0, NEG)))
    for j in range(nt):
        tau_ref[j] = jnp.broadcast_to(tau_all[:, j * LANES:(j + 1) * LANES], tau_ref.shape[1:])

    def gates(t, _):
        cols = lane_tile(t)
        s1, s2 = s1_ref[:, cols], s2_ref[:, cols]
        a_mat, b_mat, tau = lv1_ref[t], lv2_ref[t], tau_ref[t][0:1]
        a0, b0 = a_mat[0:1], b_mat[0:1]

        def n_body(q, carry):
            n1, c2 = carry
            b = lv2_ref[t, pl.ds(q, 1), :]
            n1 = n1 + jnp.where(s1 + b >= tau, 1.0, 0.0)
            c2 = c2 + jnp.where(a_mat + b >= tau, jnp.exp(b - b0), 0.0)
            return n1, c2

        n1, c2 = lax.fori_loop(0, topk, n_body, (jnp.zeros_like(s1), jnp.zeros_like(a_mat)))
        z = jnp.sum(jnp.exp(a_mat - a0) * c2, axis=0, keepdims=True)
        e2_ref[:, cols] = (jnp.exp(s2 - b0) / z).astype(BF16)
        n1_ref[0, :, cols] = n1
        e1_ref[0, :, cols] = jnp.exp(s1 - a0)
        return 0

    lax.fori_loop(0, nt, gates, 0)


def _peer_main_kernel(ht_ref, u_ref, vt_ref, r2_ref, e2_ref, n1_ref, e1_ref, x_ref, mod_ref, g_ref,
                      out_ref, acc_ref, hs0_ref, hs1_ref, act0_ref, act1_ref, r2s_ref, e2s_ref, fence_ref,
                      *, n_heads, n_keys, eb, tt, n_blocks):
    k = pl.program_id(1)
    nsub = eb // n_keys
    pk = BF16_ROWS
    grp = n_keys // pk

    @pl.when(k == 0)
    def _():
        acc_ref[...] = jnp.zeros_like(acc_ref)
        hs1_ref[...] = jnp.zeros_like(hs1_ref)
        act0_ref[...] = jnp.zeros_like(act0_ref)
        act1_ref[...] = jnp.zeros_like(act1_ref)
        r2s_ref[...] = r2_ref[...]
        e2s_ref[...] = e2_ref[...]

    def gate_tile(hs_ref, act_ref, blk, ii, tc):
        i = blk * nsub + ii
        rs = slice(ii * n_keys, (ii + 1) * n_keys)
        cs_ = slice(tc * LANES, (tc + 1) * LANES)
        nrows, erows = n1_ref[i, :, cs_], e1_ref[i, :, cs_]
        gsum = jnp.zeros((n_keys, LANES), BF16)
        for hd in range(n_heads):
            nb, ebc = (jnp.concatenate([jnp.broadcast_to(r[hd:hd + 1], (pk, LANES)).astype(BF16)] * grp, 0)
                       for r in (nrows, erows))
            ks = slice(hd * n_keys, (hd + 1) * n_keys)
            r2, e2 = r2s_ref[ks, cs_], e2s_ref[ks, cs_]
            gsum = gsum + jnp.where(r2 < nb, e2, jnp.zeros_like(e2)) * ebc
        act_ref[rs, cs_] = _gelu(hs_ref[rs, cs_]).astype(BF16) * gsum

    d = ht_ref.shape[0]
    kc = d // nsub
    ec = eb // 2
    nh = tt // 2

    def half_step(u_rows, hs_w, hs_r, act_w, act_r, blk):
        def pre_piece(ii, h):
            u_part = pltpu.bitcast(u_ref[u_rows // 2:(u_rows + eb) // 2, ii * kc:(ii + 1) * kc], BF16)
            part = _dot(u_part, ht_ref[ii * kc:(ii + 1) * kc, h * nh:(h + 1) * nh])
            if ii == 0:
                hs_w[:, h * nh:(h + 1) * nh] = part
            else:
                hs_w[:, h * nh:(h + 1) * nh] += part

        def mix_piece(j, h):
            vt_part = pltpu.bitcast(vt_ref[:, u_rows + j * ec:u_rows + (j + 1) * ec], BF16)
            acc_ref[:, h * nh:(h + 1) * nh] += _dot(vt_part, act_r[j * ec:(j + 1) * ec, h * nh:(h + 1) * nh])

        pieces = []
        for ii in range(nsub):
            pieces += [functools.partial(pre_piece, ii, h) for h in range(2)]
            if ii % 2 == 0:
                pieces += [functools.partial(mix_piece, ii // 2, h) for h in range(2)]
        tiles = [(ii, tc) for ii in range(nsub) for tc in range(tt // LANES)]
        done = 0
        for n, (ii, tc) in enumerate(tiles):
            while done < len(pieces) and done * len(tiles) <= n * len(pieces):
                pieces[done]()
                done += 1
            gate_tile(hs_r, act_w, blk, ii, tc)
        for piece in pieces[done:]:
            piece()

    half_step(0, hs0_ref, hs1_ref, act1_ref, act0_ref, jnp.maximum(2 * k - 1, 0))
    pl.semaphore_signal(fence_ref, 1)
    pl.semaphore_wait(fence_ref, 1)
    half_step(eb, hs1_ref, hs0_ref, act0_ref, act1_ref, jnp.minimum(2 * k, n_blocks - 1))

    @pl.when(k == pl.num_programs(1) - 1)
    def _():
        o = acc_ref[...].T
        out_ref[0] = x_ref[0] + mod_ref[0][5:6] * _rms(o, g_ref[...])


def _peer(x1, mod, g_pre, g_post, w_q, keys, exp_u, exp_v):
    b, l, d = x1.shape
    n_heads, _, n_keys, half = keys.shape
    n_exp = exp_u.shape[0]
    ntok = b * l
    tt = min(512, l)
    eb = 512
    tpb = l // tt
    wq_t = w_q.T.astype(BF16)
    keys_b = keys.astype(BF16)
    u_b = _pack_rows(exp_u, False)
    vt_b = _pack_rows(exp_v, True)
    tokb = lambda t, h: (t // tpb, t % tpb, 0)
    modb = lambda t, h: (t // tpb, 0, 0)
    hmap = lambda t, h: (h, 0, t)
    hshape = jax.ShapeDtypeStruct((n_heads, n_keys, ntok), F32)
    hshape_b = jax.ShapeDtypeStruct((n_heads * n_keys, ntok), BF16)
    ht, r2, e2, n1, e1 = pl.pallas_call(
        functools.partial(_peer_prep_kernel, n_keys=n_keys, topk=PEER_TOPK, tt=tt),
        grid=(ntok // tt, n_heads),
        in_specs=[pl.BlockSpec((1, tt, d), tokb), pl.BlockSpec((1,) + mod.shape[1:], modb),
                  pl.BlockSpec((1, d), lambda t, h: (0, 0)),
                  pl.BlockSpec((2 * half, d), lambda t, h: (h, 0)),
                  pl.BlockSpec((1, 2, n_keys, half), lambda t, h: (h, 0, 0, 0))],
        out_specs=[pl.BlockSpec((d, tt), lambda t, h: (0, t))] + [pl.BlockSpec((n_keys, tt), lambda t, h: (h, t))] * 2
                  + [pl.BlockSpec((1, n_keys, tt), hmap)] * 2,
        out_shape=[jax.ShapeDtypeStruct((d, ntok), BF16), hshape_b, hshape_b, hshape, hshape],
        scratch_shapes=[pltpu.VMEM((n_keys, tt), F32), pltpu.VMEM((n_keys, tt), F32),
                        pltpu.VMEM((tt // LANES, PEER_TOPK, LANES), F32), pltpu.VMEM((tt // LANES, PEER_TOPK, LANES), F32),
                        pltpu.VMEM((tt // LANES, 8, LANES), F32)],
        compiler_params=_cparams(("parallel", "arbitrary")),
        name="peer_prep",
    )(x1, mod, g_pre, wq_t, keys_b)

    n_blocks = n_exp // eb
    n_pairs = n_blocks // 2
    tokb2 = lambda t, k: (t // tpb, t % tpb, 0)
    hmap2 = lambda t, k: (0, 0, t)
    return pl.pallas_call(
        functools.partial(_peer_main_kernel, n_heads=n_heads, n_keys=n_keys, eb=eb, tt=tt, n_blocks=n_blocks),
        grid=(ntok // tt, n_pairs + 1),
        in_specs=[pl.BlockSpec((d, tt), lambda t, k: (0, t)),
                  pl.BlockSpec((eb, d), lambda t, k: (jnp.minimum(k, n_pairs - 1), 0)),
                  pl.BlockSpec((d // 2, 2 * eb), lambda t, k: (0, jnp.maximum(k - 1, 0)))]
                 + [pl.BlockSpec((n_heads * n_keys, tt), lambda t, k: (0, t))] * 2
                 + [pl.BlockSpec((n_keys, n_heads, tt), hmap2)] * 2
                 + [pl.BlockSpec((1, tt, d), tokb2),
                    pl.BlockSpec((1,) + mod.shape[1:], lambda t, k: (t // tpb, 0, 0)),
                    pl.BlockSpec((1, d), lambda t, k: (0, 0))],
        out_specs=pl.BlockSpec((1, tt, d), tokb2),
        out_shape=jax.ShapeDtypeStruct((b, l, d), F32),
        scratch_shapes=[pltpu.VMEM((d, tt), F32), pltpu.VMEM((eb, tt), F32), pltpu.VMEM((eb, tt), F32),
                        pltpu.VMEM((eb, tt), BF16), pltpu.VMEM((eb, tt), BF16),
                        pltpu.VMEM((n_heads * n_keys, tt), BF16), pltpu.VMEM((n_heads * n_keys, tt), BF16),
                        pltpu.SemaphoreType.REGULAR],
        compiler_params=_cparams(("parallel", "arbitrary")),
        name="peer_main",
    )(ht, u_b, vt_b, r2, e2, n1.transpose(1, 0, 2), e1.transpose(1, 0, 2), x1, mod, g_post)


def kernel(x, c, ctx, c_ctx, w_mod, b_mod, g_pre_mix, g_post_mix, g_pre_ffn, g_post_ffn, w_in, w_out,
           s5_lam_re, s5_lam_im, s5_log_dt, s5_b_re, s5_b_im, s5_c_re, s5_c_im, s5_d, s5_w_glu,
           gdn_conv, gdn_a_log, gdn_dt_bias, gdn_norm, peer_w_q, peer_keys, peer_u, peer_v):
    assert w_mod.shape[0] == 1, "single layer: the context stream only feeds the latent scans"
    b, l, d = x.shape
    lc = ctx.shape[1]
    n_heads = gdn_a_log.shape[-1]
    head_dim = gdn_norm.shape[-1]
    gdn_w = n_heads * head_dim
    s5_w = s5_d.shape[-1]
    n_gate = 2 * 2 * n_heads

    n_cond = -(-(b + 1) // 8) * 8
    cond = jnp.concatenate([c, c_ctx[None], jnp.zeros((n_cond - b - 1, d), F32)], axis=0)
    mod = _adaln(cond, w_mod[0], b_mod[0]).reshape(n_cond, 6, d)

    n_proj = w_in.shape[-1]
    w_pad = jnp.pad(w_in[0], ((0, 0), (0, -n_proj % LANES))).astype(BF16)
    g1 = g_pre_mix[0][None]
    ux, qkvx, zx, abx = _premix(x, mod[:b], True, g1, w_pad, s5_w, 3 * gdn_w, gdn_w)
    uc, qkvc, _, abc = _premix(ctx, mod[b:b + 1], False, g1, w_pad, s5_w, 3 * gdn_w, gdn_w)
    assert abx.shape[-1] == LANES and n_gate <= LANES

    mats = _s5_matrices(s5_lam_re[0], s5_lam_im[0], s5_log_dt[0], s5_b_re[0], s5_b_im[0],
                        s5_c_re[0], s5_c_im[0], s5_d[0])
    y_s5 = _s5(ux, uc, mats)

    qkvx = _gdn_conv(qkvx, gdn_conv[0], GRID_W, l // GRID_W > 1, n_heads, head_dim)
    qkvc = _gdn_conv(qkvc, gdn_conv[0], lc, False, n_heads, head_dim)
    o_gdn = _gdn(qkvx, abx, qkvc, abc, gdn_a_log[0], gdn_dt_bias[0], n_heads, head_dim)

    x1 = _postmix(x, y_s5, o_gdn, zx, mod[:b], gdn_norm[0][None], g_post_mix[0][None],
                  s5_w_glu[0].astype(BF16), w_out[0].astype(BF16), n_heads, head_dim)
    return _peer(x1, mod[:b], g_pre_ffn[0][None], g_post_ffn[0][None],
                 peer_w_q[0], peer_keys[0], peer_u[0], peer_v[0])
```

```python
import functools

import jax
import jax.numpy as jnp
from jax import lax
from jax.experimental import pallas as pl
from jax.experimental.pallas import tpu as pltpu

F32 = jnp.float32
BF16 = jnp.bfloat16
HI = lax.Precision.HIGHEST

EPS = 1e-6
GRID_W = 64
GDN_CHUNK = 64
S5_CHUNK = 8
PEER_TOPK = 16
LANES = 128
BF16_ROWS = 16
NEG = -1e30
VMEM_LIMIT = 56 * 1024 * 1024


def _cparams(sem):
    return pltpu.CompilerParams(dimension_semantics=sem, vmem_limit_bytes=VMEM_LIMIT)


def _sigmoid(x):
    return 1.0 / (1.0 + jnp.exp(-x))


def _softplus(x):
    return jnp.maximum(x, 0.0) + jnp.log(1.0 + jnp.exp(-jnp.abs(x)))


def _gelu(x):
    return 0.5 * x * (1.0 + lax.erf(x * (2.0 ** -0.5)))


def _rms(x, g):
    return x * lax.rsqrt(jnp.mean(x * x, axis=-1, keepdims=True) + EPS) * g


def _dot(a, b, dims=None, precision=None):
    if dims is None:
        dims = (((a.ndim - 1,), (0,)), ((), ()))
    return lax.dot_general(a, b, dims, precision=precision, preferred_element_type=F32)


def _pack_kernel(x_ref, o_ref, *, transpose):
    x = x_ref[...]
    x = x.T if transpose else x
    o_ref[...] = pltpu.bitcast(x.astype(BF16), jnp.uint32)


def _pack_rows(x, transpose):
    n, m = x.shape
    tn = 512
    if transpose:
        out_shape, out_spec = (m // 2, n), pl.BlockSpec((m // 2, tn), lambda i: (0, i))
    else:
        out_shape, out_spec = (n // 2, m), pl.BlockSpec((tn // 2, m), lambda i: (i, 0))
    return pl.pallas_call(
        functools.partial(_pack_kernel, transpose=transpose),
        grid=(n // tn,),
        in_specs=[pl.BlockSpec((tn, m), lambda i: (i, 0))],
        out_specs=out_spec,
        out_shape=jax.ShapeDtypeStruct(out_shape, jnp.uint32),
        compiler_params=_cparams(("parallel",)),
        name="pack_table",
    )(x)


_NT = (((1,), (1,)), ((), ()))
_TN = (((0,), (0,)), ((), ()))


def _adaln_kernel(c_ref, w_ref, b_ref, o_ref):
    c = c_ref[...]
    o_ref[...] = _dot(c * _sigmoid(c), w_ref[...], precision=HI) + b_ref[...]


def _adaln(cond, w_mod, b_mod):
    n, d = cond.shape
    m = w_mod.shape[1]
    tn = 512
    return pl.pallas_call(
        _adaln_kernel,
        grid=(m // tn,),
        in_specs=[pl.BlockSpec((n, d), lambda j: (0, 0)),
                  pl.BlockSpec((d, tn), lambda j: (0, j)),
                  pl.BlockSpec((1, tn), lambda j: (0, j))],
        out_specs=pl.BlockSpec((n, tn), lambda j: (0, j)),
        out_shape=jax.ShapeDtypeStruct((n, m), F32),
        compiler_params=_cparams(("arbitrary",)),
        name="adaln",
    )(cond, w_mod, b_mod.reshape(1, m))


def _premix_kernel(x_ref, mod_ref, g_ref, w_ref, u_ref, qkv_ref, z_ref, ab_ref, *, n_u, n_qkv, n_z):
    m = mod_ref[0]
    h = _rms(x_ref[0], g_ref[...]) * (1.0 + m[1:2]) + m[0:1]
    p = _dot(h.astype(BF16), w_ref[...])
    for j in range(u_ref.shape[0]):
        u_ref[j] = p[:, j * LANES:(j + 1) * LANES].reshape(u_ref.shape[1:])
    qkv_ref[0] = p[:, n_u:n_u + n_qkv]
    z_ref[0] = p[:, n_u + n_qkv:n_u + n_qkv + n_z].astype(BF16)
    ab_ref[0] = p[:, n_u + n_qkv + n_z:]


def _premix(x, mod, per_batch, g, w_pad, n_u, n_qkv, n_z):
    b, l, d = x.shape
    tm = min(512, l)
    tc = S5_CHUNK
    n_all = w_pad.shape[1]
    n_ab = n_all - n_u - n_qkv - n_z
    mod_map = (lambda i, t: (i, 0, 0)) if per_batch else (lambda i, t: (0, 0, 0))
    tok = lambda i, t: (i, t, 0)
    return pl.pallas_call(
        functools.partial(_premix_kernel, n_u=n_u, n_qkv=n_qkv, n_z=n_z),
        grid=(b, l // tm),
        in_specs=[pl.BlockSpec((1, tm, d), tok),
                  pl.BlockSpec((1,) + mod.shape[1:], mod_map),
                  pl.BlockSpec((1, d), lambda i, t: (0, 0)),
                  pl.BlockSpec((d, n_all), lambda i, t: (0, 0))],
        out_specs=[pl.BlockSpec((n_u // LANES, tm // tc, None, tc, LANES), lambda i, t: (0, t, i, 0, 0)),
                   pl.BlockSpec((1, tm, n_qkv), tok),
                   pl.BlockSpec((1, tm, n_z), tok), pl.BlockSpec((1, tm, n_ab), tok)],
        out_shape=[jax.ShapeDtypeStruct((n_u // LANES, l // tc, b, tc, LANES), F32), jax.ShapeDtypeStruct((b, l, n_qkv), F32),
                   jax.ShapeDtypeStruct((b, l, n_z), BF16), jax.ShapeDtypeStruct((b, l, n_ab), F32)],
        compiler_params=_cparams(("parallel", "parallel")),
        name="premix",
    )(x, mod, g, w_pad)


def _s5_matrices(lam_re, lam_im, log_dt, b_re, b_im, c_re, c_im, d_skip):
    t = S5_CHUNK
    n_dirs, g, p = lam_re.shape
    cg = b_re.shape[-1]
    dt = jnp.exp(log_dt)[..., None]
    lr, li = lam_re, lam_im
    den = lr * lr + li * li
    mag = jnp.exp(lr * dt)
    lb_re, lb_im = mag * jnp.cos(li * dt), mag * jnp.sin(li * dt)
    nr = lb_re - 1.0
    cr = (nr * lr + lb_im * li) / den
    ci = (lb_im * lr - nr * li) / den
    bb_re = cr[..., None] * b_re - ci[..., None] * b_im
    bb_im = cr[..., None] * b_im + ci[..., None] * b_re
    taus = jnp.arange(t + 1, dtype=F32)[:, None, None, None]
    pmag = jnp.exp(taus * (lr * dt))
    pw_re = pmag * jnp.cos(taus * (li * dt))
    pw_im = pmag * jnp.sin(taus * (li * dt))
    ein = functools.partial(jnp.einsum, precision=HI)
    lbr = pw_re[..., None] * bb_re - pw_im[..., None] * bb_im
    lbi = pw_re[..., None] * bb_im + pw_im[..., None] * bb_re
    kk = ein('dgcp,tdgpe->tdgce', c_re, lbr) - ein('dgcp,tdgpe->tdgce', c_im, lbi)
    er = c_re[None] * pw_re[:, :, :, None, :] - c_im[None] * pw_im[:, :, :, None, :]
    ei = c_re[None] * pw_im[:, :, :, None, :] + c_im[None] * pw_re[:, :, :, None, :]

    s_idx = jnp.arange(t)
    lag = s_idx[None, :] - s_idx[:, None]
    kf = kk[:, 0][jnp.clip(lag, 0, t)]
    kb = kk[:, 1][jnp.clip(-lag, 0, t)]
    m_full = (jnp.where((lag >= 0)[:, :, None, None, None], kf, 0.0)
              + jnp.where((lag <= 0)[:, :, None, None, None], kb, 0.0))
    eye_t = jnp.eye(t, dtype=F32)[:, :, None, None, None]
    eye_c = jnp.eye(cg, dtype=F32)[None, None, None]
    m_full = m_full + eye_t * eye_c * d_skip.reshape(g, cg)[None, None, :, :, None]
    p_parts = (lbr[:t, 0][::-1], lbi[:t, 0][::-1], lbr[:t, 1], lbi[:t, 1])
    r_parts = (er[1:, 0], -ei[1:, 0], er[1:, 1][::-1], -ei[1:, 1][::-1])

    gw = LANES // cg
    n8 = g // gw
    def spread(width):
        lane = jnp.arange(gw * width)
        tile = (lane[None, :] % width == jnp.arange(width)[:, None]).astype(F32)
        own = (lane[None, :] // width == jnp.arange(gw)[:, None]).astype(F32)
        return tile, own[None, None, :, None, None, :]

    tile_c, own_c = spread(cg)
    tile_p, own_p = spread(p)
    m6 = m_full.reshape(t, t, n8, gw, cg, cg).transpose(2, 0, 3, 5, 1, 4)
    mm = (ein('xsgctd,dl->xsgctl', m6, tile_c) * own_c).reshape(n8, t * LANES, t * LANES)
    p7 = jnp.stack([a.reshape(t, n8, gw, p, cg).transpose(1, 0, 2, 4, 3) for a in p_parts], axis=4)
    pq = (ein('xsgcqn,nl->xsgcql', p7, tile_p) * own_p).reshape(n8, t * LANES, 4 * gw * p)
    r7 = jnp.stack([a.reshape(t, n8, gw, cg, p).transpose(1, 2, 4, 0, 3) for a in r_parts], axis=1)
    rr = (ein('xqgntd,dl->xqgntl', r7, tile_c) * own_c).reshape(n8, 4 * gw * p, t * LANES)
    dec = jnp.stack([x.reshape(n8, gw * p) for x in (pw_re[t, 0], pw_im[t, 0], pw_re[t, 1], pw_im[t, 1])], axis=1)
    dec = jnp.concatenate([dec, jnp.zeros_like(dec)], axis=1)
    return pq.astype(BF16), mm.astype(BF16), rr.astype(BF16), dec


def _chunk_rows(ref, nb, t):
    n = ref.shape[0]
    return jnp.concatenate([ref[:, :, s, :].reshape(n * nb, LANES) for s in range(t)], axis=1).astype(BF16)


def _s5_state_kernel(x_ref, c_ref, pq_ref, dec_ref, sin_ref, loc_ref, locc_ref, *, nb, t, nct, ncx, ncc, sw):
    rt = pl.program_id(1)
    rows = nct * nb
    loc_ref[pl.ds(pl.multiple_of(rt * rows, rows), rows), :] = _dot(_chunk_rows(x_ref, nb, t), pq_ref[0])

    @pl.when(rt == 0)
    def _():
        locc_ref[...] = _dot(_chunk_rows(c_ref, nb, t), pq_ref[0])

    @pl.when(rt == pl.num_programs(1) - 1)
    def _():
        dec = dec_ref[0]
        far, fai, bar, bai = (jnp.broadcast_to(dec[i:i + 1], (nb, sw)) for i in range(4))

        def step(state, ar, ai, ref, k, c0):
            re, im = state
            r = pl.ds(pl.multiple_of(k * nb, nb), nb)
            return ar * re - ai * im + ref[r, c0:c0 + sw], ar * im + ai * re + ref[r, c0 + sw:c0 + 2 * sw]

        zero = jnp.zeros((nb, sw), F32)

        def ctx_body(i, carry):
            f, bk = carry
            return step(f, far, fai, locc_ref, i, 0), step(bk, bar, bai, locc_ref, ncc - 1 - i, 2 * sw)

        carry = lax.fori_loop(0, ncc, ctx_body, ((zero, zero), (zero, zero)))

        def pair_store(k0, c0, s_lo, s_hi):
            r = pl.ds(pl.multiple_of(k0 * nb, 2 * nb), 2 * nb)
            sin_ref[0, r, c0:c0 + sw] = jnp.concatenate([s_lo[0], s_hi[0]], axis=0).astype(BF16)
            sin_ref[0, r, c0 + sw:c0 + 2 * sw] = jnp.concatenate([s_lo[1], s_hi[1]], axis=0).astype(BF16)

        def x_body(j, carry):
            f0, b0 = carry
            f1 = step(f0, far, fai, loc_ref, 2 * j, 0)
            pair_store(2 * j, 0, f0, f1)
            kb = ncx - 2 - 2 * j
            b1 = step(b0, bar, bai, loc_ref, kb + 1, 2 * sw)
            pair_store(kb, 2 * sw, b1, b0)
            return step(f1, far, fai, loc_ref, 2 * j + 1, 0), step(b1, bar, bai, loc_ref, kb, 2 * sw)

        lax.fori_loop(0, ncx // 2, x_body, carry)


def _s5_out_kernel(x_ref, sin_ref, mm_ref, rr_ref, y_ref, *, nb, t):
    n = x_ref.shape[0]
    y = _dot(_chunk_rows(x_ref, nb, t), mm_ref[0]) + _dot(sin_ref[0], rr_ref[0])
    for s in range(t):
        y_ref[:, :, s, :] = y[:, s * LANES:(s + 1) * LANES].reshape(n, nb, LANES)


def _s5(ux, uc, mats):
    pq, mm, rr, dec = mats
    n8, ncx, nb, t, _ = ux.shape
    ncc = uc.shape[1]
    sw = dec.shape[-1]
    assert nb % 8 == 0 and t == S5_CHUNK and ncx % 2 == 0 and n8 == pq.shape[0]
    nct = min(64, ncx)
    ux5, uc5 = ux, uc
    kdim = t * LANES
    xblk = pl.BlockSpec((None, nct, nb, t, LANES), lambda g, r: (g, r, 0, 0, 0))
    wmap = lambda g, r: (g, 0, 0)
    sin = pl.pallas_call(
        functools.partial(_s5_state_kernel, nb=nb, t=t, nct=nct, ncx=ncx, ncc=ncc, sw=sw),
        grid=(n8, ncx // nct),
        in_specs=[xblk, pl.BlockSpec((None, ncc, nb, t, LANES), lambda g, r: (g, 0, 0, 0, 0)),
                  pl.BlockSpec((1, kdim, 4 * sw), wmap), pl.BlockSpec((1, 8, sw), wmap)],
        out_specs=pl.BlockSpec((1, ncx * nb, 4 * sw), wmap),
        out_shape=jax.ShapeDtypeStruct((n8, ncx * nb, 4 * sw), BF16),
        scratch_shapes=[pltpu.VMEM((ncx * nb, 4 * sw), F32), pltpu.VMEM((ncc * nb, 4 * sw), F32)],
        compiler_params=_cparams(("parallel", "arbitrary")),
        name="s5_state",
    )(ux5, uc5, pq, dec)
    y = pl.pallas_call(
        functools.partial(_s5_out_kernel, nb=nb, t=t),
        grid=(n8, ncx // nct),
        in_specs=[xblk, pl.BlockSpec((1, nct * nb, 4 * sw), lambda g, r: (g, r, 0)),
                  pl.BlockSpec((1, kdim, kdim), wmap), pl.BlockSpec((1, 4 * sw, kdim), wmap)],
        out_specs=xblk,
        out_shape=jax.ShapeDtypeStruct((n8, ncx, nb, t, LANES), F32),
        compiler_params=_cparams(("parallel", "parallel")),
        name="s5_out",
    )(ux5, sin, mm, rr)
    return y


_CONV_PAD = 72


def _gdn_conv_kernel(x_ref, w_ref, o_ref, pad_ref, *, length, cols, two_d, n_norm, n_q, q_scale):
    cb = pl.program_id(1)
    pad = _CONV_PAD
    zeros = jnp.zeros((pad, LANES), F32)
    pad_ref[0:pad, :] = zeros
    pad_ref[pad + length:pad + length + pad, :] = zeros
    pad_ref[pad:pad + length, :] = x_ref[0]
    rc = min(256, length)
    for r0 in range(0, length, rc):
        col = lax.rem(lax.broadcasted_iota(jnp.int32, (rc, LANES), 0) + r0, cols)
        acc = jnp.zeros((rc, LANES), F32)
        for dr in ((0, 1, 2) if two_d else (1,)):
            for dc in range(3):
                off = (dr - 1) * cols + (dc - 1)
                term = pad_ref[pad + r0 + off:pad + r0 + off + rc, :] * w_ref[dr * 3 + dc:dr * 3 + dc + 1, :]
                if dc == 0:
                    term = jnp.where(col >= 1, term, 0.0)
                if dc == 2:
                    term = jnp.where(col <= cols - 2, term, 0.0)
                acc = acc + term
        y = acc * _sigmoid(acc)
        nrm = y * lax.rsqrt(jnp.sum(y * y, axis=-1, keepdims=True) + EPS)
        nrm = nrm * jnp.where(cb < n_q, q_scale, 1.0)
        o_ref[0, r0:r0 + rc, :] = jnp.where(cb < n_norm, nrm, y)


def _gdn_conv(qkv, conv_w, cols, two_d, n_heads, head_dim):
    b, l, ch = qkv.shape
    w9 = jnp.concatenate([conv_w.reshape(9, ch), jnp.zeros((7, ch), F32)], axis=0)
    blk = lambda i, j: (i, 0, j)
    return pl.pallas_call(
        functools.partial(_gdn_conv_kernel, length=l, cols=cols, two_d=two_d,
                          n_norm=2 * n_heads, n_q=n_heads, q_scale=head_dim ** -0.5),
        grid=(b, ch // LANES),
        in_specs=[pl.BlockSpec((1, l, LANES), blk), pl.BlockSpec((16, LANES), lambda i, j: (0, j))],
        out_specs=pl.BlockSpec((1, l, LANES), blk),
        out_shape=jax.ShapeDtypeStruct((b, l, ch), F32),
        scratch_shapes=[pltpu.VMEM((l + 2 * _CONV_PAD, LANES), F32)],
        compiler_params=_cparams(("parallel", "parallel")),
        name="gdn_conv",
    )(qkv, w9)


def _gdn_kernel(qx_ref, kx_ref, vx_ref, abx_ref, qc_ref, kc_ref, vc_ref, abc_ref,
                pl_ref, o_ref, gcx_ref, bcx_ref, gcc_ref, bcc_ref,
                u_ref, w_ref, qd_ref, kd_ref, at_ref, gt_ref, *, n_heads, ncx, ncc):
    cs = GDN_CHUNK
    hd = qx_ref.shape[-1]
    head = pl.program_id(1)
    alog_l, dtb_l = pl_ref[0:1, :], pl_ref[1:2, :]
    lane = lax.broadcasted_iota(jnp.int32, (1, LANES), 1)

    def gate_columns(ab_ref, g_out, b_out, n_rows):
        rc = min(256, n_rows)
        for r0 in range(0, n_rows, rc):
            ab = ab_ref[0, r0:r0 + rc, :]
            g_all = -jnp.exp(alog_l) * _softplus(ab + dtb_l)
            b_all = _sigmoid(ab)
            for dr in range(2):
                j = dr * n_heads + head
                g = jnp.sum(jnp.where(lane == j, g_all, 0.0), axis=-1, keepdims=True)
                bt = jnp.sum(jnp.where(lane == j + 2 * n_heads, b_all, 0.0), axis=-1, keepdims=True)
                g_out[dr, r0:r0 + rc, :] = jnp.broadcast_to(g, (rc, hd))
                b_out[dr, r0:r0 + rc, :] = jnp.broadcast_to(bt, (rc, hd))

    gate_columns(abx_ref, gcx_ref, bcx_ref, ncx * cs)
    gate_columns(abc_ref, gcc_ref, bcc_ref, ncc * cs)

    nblk = 4
    side = nblk * cs
    def block_masks():
        ri = lax.broadcasted_iota(jnp.int32, (side, side), 0)
        ci = lax.broadcasted_iota(jnp.int32, (side, side), 1)
        dist = jnp.where(ri < side // 2, ri - ci, ci - ri)
        dist = jnp.where((ri // cs) == (ci // cs), dist, -1)
        return dist >= 0, dist > 0, jnp.where(ri == ci, 1.0, 0.0)

    def cumsum_blocks(x):
        row_id = lax.broadcasted_iota(jnp.int32, (side, hd), 0)
        pos = row_id % cs
        is_fwd = row_id < side // 2
        s = 1
        while s < cs:
            x = x + jnp.where(is_fwd, jnp.where(pos >= s, pltpu.roll(x, s, 0), 0.0),
                              jnp.where(pos < cs - s, pltpu.roll(x, side - s, 0), 0.0))
            s *= 2
        return x

    def each(f, *lists):
        return [f(*args) for args in zip(*lists)]

    def chunks_local(c0s, refs, base):
        q_ref, k_ref, v_ref, gcol_ref, bcol_ref = refs
        incl, strict, eye = block_masks()
        rows = [pl.ds(pl.multiple_of(c0 * cs, 2 * cs), 2 * cs) for c0 in c0s]
        q, k, v = ([jnp.concatenate([r[0, rw, :]] * 2, axis=0) for rw in rows] for r in (q_ref, k_ref, v_ref))
        gb, bb = ([jnp.concatenate([r[0, rw, :], r[1, rw, :]], axis=0) for rw in rows] for r in (gcol_ref, bcol_ref))
        gcs = each(cumsum_blocks, gb)
        gwide = each(lambda g: jnp.concatenate([g] * (side // hd), axis=1), gcs)
        decay = each(lambda g: jnp.where(incl, jnp.exp(jnp.where(incl, g - g.T[0:1, :], 0.0)), 0.0), gwide)
        kbf = each(lambda t: t.astype(BF16), k)
        kb = each(lambda t, b_: t * b_, k, bb)
        lmat = each(lambda x, y, dc: jnp.where(strict, _dot(x.astype(BF16), y, _NT) * dc, 0.0), kb, kbf, decay)
        pw = each(lambda l_: -l_, lmat)
        tmat = each(lambda p_: eye + p_, pw)
        for _ in range(cs.bit_length() - 2):
            pw = each(lambda p_: _dot(p_.astype(BF16), p_.astype(BF16)), pw)
            tmat = each(lambda t_, p_: t_ + _dot(t_.astype(BF16), p_.astype(BF16)), tmat, pw)
        g_exp = each(jnp.exp, gcs)
        rhs = each(lambda v_, b_, kb_, ge: jnp.concatenate([v_ * b_, kb_ * ge], axis=1).astype(BF16), v, bb, kb, g_exp)
        uw = each(lambda t_, r_: _dot(t_.astype(BF16), r_), tmat, rhs)
        attn = each(lambda q_, y, dc: jnp.where(incl, _dot(q_.astype(BF16), y, _NT) * dc, 0.0), q, kbf, decay)
        half = side // 2
        for c0, uw_, attn_, q_, k_, g_, ge in zip(c0s, uw, attn, q, k, gcs, g_exp):
            lasts = [g_[(j + 1) * cs - 1:(j + 1) * cs, :] if j < nblk // 2 else g_[j * cs:j * cs + 1, :] for j in range(nblk)]
            g_last = jnp.concatenate([jnp.broadcast_to(r, (cs, hd)) for r in lasts], axis=0)
            qd = (q_ * ge).astype(BF16)
            kd = (k_ * jnp.exp(g_last - g_)).astype(BF16)
            srows = pl.ds(pl.multiple_of((base + c0) * cs, 2 * cs), 2 * cs)
            grows = pl.ds(pl.multiple_of((base + c0) * 8, 16), 16)
            for dr in range(2):
                blk = slice(dr * half, (dr + 1) * half)
                u_ref[dr, srows, :] = uw_[blk, :hd]
                w_ref[dr, srows, :] = uw_[blk, hd:].astype(BF16)
                qd_ref[dr, srows, :] = qd[blk]
                kd_ref[dr, srows, :] = kd[blk]
                at_ref[dr, srows, :] = jnp.concatenate(
                    [attn_[j * cs:(j + 1) * cs, j * cs:(j + 1) * cs] for j in (2 * dr, 2 * dr + 1)], axis=0).astype(BF16)
                gt_ref[dr, grows, :] = jnp.concatenate(
                    [jnp.broadcast_to(jnp.exp(lasts[j]), (8, hd)) for j in (2 * dr, 2 * dr + 1)], axis=0)

    def chunks_seq(cgs, states):
        drs = (0, 1)
        srows = [pl.ds(pl.multiple_of(cg * cs, cs), cs) for cg in cgs]
        sb = [st.astype(BF16) for st in states]
        vb = [(u_ref[dr, r, :] - _dot(w_ref[dr, r, :], s_)).astype(BF16) for dr, r, s_ in zip(drs, srows, sb)]
        new = [st * gt_ref[dr, pl.ds(pl.multiple_of(cg * 8, 8), 8), :][0:1] + _dot(kd_ref[dr, r, :], v_, _TN)
               for dr, cg, r, st, v_ in zip(drs, cgs, srows, states, vb)]
        outs = [_dot(qd_ref[dr, r, :], s_) + _dot(at_ref[dr, r, :], v_) for dr, r, s_, v_ in zip(drs, srows, sb, vb)]
        return outs, new

    refs_c = (qc_ref, kc_ref, vc_ref, gcc_ref, bcc_ref)
    refs_x = (qx_ref, kx_ref, vx_ref, gcx_ref, bcx_ref)

    def local_pass(n, refs, base):
        def body(i, _):
            chunks_local([4 * i, 4 * i + 2], refs, base)
            return 0

        lax.fori_loop(0, n // 4, body, 0)

    local_pass(ncc, refs_c, 0)
    local_pass(ncx, refs_x, ncc)

    zero = jnp.zeros((hd, hd), F32)

    def ctx_body(i, carry):
        return tuple(chunks_seq((i, ncc - 1 - i), carry)[1])

    carry = lax.fori_loop(0, ncc, ctx_body, (zero, zero))
    o_ref[...] = jnp.zeros_like(o_ref)

    def x_body(i, carry):
        ib = ncx - 1 - i
        (of, ob), new = chunks_seq((ncc + i, ncc + ib), carry)
        o_ref[0, pl.ds(pl.multiple_of(i * cs, cs), cs), :] += of
        o_ref[0, pl.ds(pl.multiple_of(ib * cs, cs), cs), :] += ob
        return tuple(new)

    lax.fori_loop(0, ncx, x_body, carry)


def _gdn(qkvx, abx, qkvc, abc, a_log, dt_bias, n_heads, head_dim):
    b, l, _ = qkvx.shape
    lc = qkvc.shape[1]
    cs = GDN_CHUNK
    ncx, ncc = l // cs, lc // cs
    assert ncx % 4 == 0 and ncc % 4 == 0, "chunks are processed four at a time"
    ng = 2 * n_heads
    p_l = jnp.zeros((8, LANES), F32)
    p_l = p_l.at[0, :ng].set(a_log.reshape(ng)).at[1, :ng].set(dt_bias.reshape(ng))

    def col(off):
        return lambda i, h: (i, 0, off + h)

    full = lambda i, h: (i, 0, 0)
    const = lambda i, h: (0, 0)
    hb = head_dim
    return pl.pallas_call(
        functools.partial(_gdn_kernel, n_heads=n_heads, ncx=ncx, ncc=ncc),
        grid=(b, n_heads),
        in_specs=[pl.BlockSpec((1, l, hb), col(0)), pl.BlockSpec((1, l, hb), col(n_heads)),
                  pl.BlockSpec((1, l, hb), col(2 * n_heads)),
                  pl.BlockSpec((1, l, LANES), full),
                  pl.BlockSpec((1, lc, hb), col(0)), pl.BlockSpec((1, lc, hb), col(n_heads)),
                  pl.BlockSpec((1, lc, hb), col(2 * n_heads)),
                  pl.BlockSpec((1, lc, LANES), full),
                  pl.BlockSpec((8, LANES), const)],
        out_specs=pl.BlockSpec((1, l, hb), col(0)),
        out_shape=jax.ShapeDtypeStruct((b, l, n_heads * head_dim), F32),
        scratch_shapes=[pltpu.VMEM((2, l, hb), F32), pltpu.VMEM((2, l, hb), F32),
                        pltpu.VMEM((2, lc, hb), F32), pltpu.VMEM((2, lc, hb), F32),
                        pltpu.VMEM((2, l + lc, hb), F32), pltpu.VMEM((2, l + lc, hb), BF16),
                        pltpu.VMEM((2, l + lc, hb), BF16), pltpu.VMEM((2, l + lc, hb), BF16),
                        pltpu.VMEM((2, l + lc, cs), BF16), pltpu.VMEM((2, (ncx + ncc) * 8, hb), F32)],
        compiler_params=_cparams(("parallel", "arbitrary")),
        name="gdn",
    )(qkvx, qkvx, qkvx, abx, qkvc, qkvc, qkvc, abc, p_l)


def _postmix_kernel(x_ref, y_ref, o_ref, z_ref, mod_ref, gn_ref, gpost_ref, wglu_ref, wout_ref, out_ref,
                    *, n_heads, head_dim, s5_w):
    s = _gelu(jnp.concatenate([y_ref[j].reshape(x_ref.shape[1], LANES) for j in range(y_ref.shape[0])], axis=1))
    s = s * _sigmoid(_dot(s.astype(BF16), wglu_ref[...]))
    o = o_ref[0]
    z = z_ref[0].astype(F32)
    parts = []
    for h in range(n_heads):
        sl = slice(h * head_dim, (h + 1) * head_dim)
        zh = z[:, sl]
        parts.append(_rms(o[:, sl], gn_ref[...]) * (zh * _sigmoid(zh)))
    gd = jnp.concatenate(parts, axis=-1)
    mixed = _dot(s.astype(BF16), wout_ref[0:s5_w, :]) + _dot(gd.astype(BF16), wout_ref[s5_w:, :])
    gate = mod_ref[0][2:3]
    out_ref[0] = x_ref[0] + gate * _rms(mixed, gpost_ref[...])


def _postmix(x, y, o, z, mod, gdn_norm, g_post, w_glu, w_out, n_heads, head_dim):
    b, l, d = x.shape
    s5_w = y.shape[0] * y.shape[-1]
    gw = o.shape[-1]
    tm = min(512, l)
    tc = S5_CHUNK
    tok = lambda i, t: (i, t, 0)
    const = lambda i, t: (0, 0)
    return pl.pallas_call(
        functools.partial(_postmix_kernel, n_heads=n_heads, head_dim=head_dim, s5_w=s5_w),
        grid=(b, l // tm),
        in_specs=[pl.BlockSpec((1, tm, d), tok), pl.BlockSpec((s5_w // LANES, tm // tc, None, tc, LANES), lambda i, t: (0, t, i, 0, 0)),
                  pl.BlockSpec((1, tm, gw), tok), pl.BlockSpec((1, tm, gw), tok),
                  pl.BlockSpec((1,) + mod.shape[1:], lambda i, t: (i, 0, 0)),
                  pl.BlockSpec((1, head_dim), const), pl.BlockSpec((1, d), const),
                  pl.BlockSpec((s5_w, s5_w), const), pl.BlockSpec((s5_w + gw, d), const)],
        out_specs=pl.BlockSpec((1, tm, d), tok),
        out_shape=jax.ShapeDtypeStruct((b, l, d), F32),
        compiler_params=_cparams(("parallel", "parallel")),
        name="postmix",
    )(x, y, o, z, mod, gdn_norm, g_post, w_glu, w_out)


def _peer_prep_kernel(x_ref, mod_ref, g_ref, wq_ref, keys_ref, ht_ref, r2_ref, e2_ref, n1_ref, e1_ref,
                      s1_ref, s2_ref, lv1_ref, lv2_ref, tau_ref, *, n_keys, topk, tt):
    @pl.when(pl.program_id(1) == 0)
    def _():
        m = mod_ref[0]
        h = _rms(x_ref[0], g_ref[...]) * (1.0 + m[4:5]) + m[3:4]
        ht_ref[...] = h.T.astype(BF16)

    half = keys_ref.shape[-1]
    ht = ht_ref[...]
    q1 = _dot(wq_ref[0:half, :], ht).astype(BF16)
    q2 = _dot(wq_ref[half:2 * half, :], ht).astype(BF16)
    s1_ref[...] = _dot(keys_ref[0, 0], q1)
    s2_ref[...] = _dot(keys_ref[0, 1], q2)
    kf = float(topk)

    nt = tt // LANES

    def lane_tile(t):
        return pl.ds(pl.multiple_of(t * LANES, LANES), LANES)

    def levels(t, _):
        cols = lane_tile(t)

        def lv_body(q, carry):
            c1, c2, rank2 = carry
            m1 = jnp.max(c1, axis=0, keepdims=True)
            m2 = jnp.max(c2, axis=0, keepdims=True)
            lv1_ref[t, pl.ds(q, 1), :] = m1
            lv2_ref[t, pl.ds(q, 1), :] = m2
            h1 = c1 == jnp.where(m1 > NEG, m1, -NEG)
            h2 = c2 == jnp.where(m2 > NEG, m2, -NEG)
            rank2 = jnp.where(h2, jnp.asarray(q, F32), rank2)
            return jnp.where(h1, NEG, c1), jnp.where(h2, NEG, c2), rank2

        s1, s2 = s1_ref[:, cols], s2_ref[:, cols]
        _, _, rank2 = lax.fori_loop(0, topk, lv_body, (s1, s2, jnp.full(s2.shape, kf, F32)))
        r2_ref[:, cols] = rank2.astype(BF16)
        return 0

    lax.fori_loop(0, nt, levels, 0)

    a_all, b_all = (jnp.concatenate([r[j] for j in range(nt)], axis=1) for r in (lv1_ref, lv2_ref))
    sub = topk // 2
    cand = jnp.concatenate([a_all[0:1] + b_all] + [a_all[p:p + 1] + b_all[0:sub] for p in range(1, sub)]
                           + [a_all[sub:] + b_all[0:1]], axis=0)

    def tau_body(_, carry):
        cand, cnt, tau = carry
        mx = jnp.max(cand, axis=0, keepdims=True)
        hit = cand == mx
        tau = jnp.where(cnt < kf, mx, tau)
        cnt = cnt + jnp.sum(jnp.where(hit, 1.0, 0.0), axis=0, keepdims=True)
        return jnp.where(hit, NEG, cand), cnt, tau

    row0 = a_all[0:1]
    _, _, tau_all = lax.fori_loop(0, topk, tau_body, (cand, jnp.zeros_like(row0), jnp.full_like(row0, NEG)))
    for j in range(nt):
        tau_ref[j] = jnp.broadcast_to(tau_all[:, j * LANES:(j + 1) * LANES], tau_ref.shape[1:])

    def gates(t, _):
        cols = lane_tile(t)
        s1, s2 = s1_ref[:, cols], s2_ref[:, cols]
        a_mat, b_mat, tau = lv1_ref[t], lv2_ref[t], tau_ref[t][0:1]
        a0, b0 = a_mat[0:1], b_mat[0:1]

        def n_body(q, carry):
            n1, c2 = carry
            b = lv2_ref[t, pl.ds(q, 1), :]
            n1 = n1 + jnp.where(s1 + b >= tau, 1.0, 0.0)
            c2 = c2 + jnp.where(a_mat + b >= tau, jnp.exp(b - b0), 0.0)
            return n1, c2

        n1, c2 = lax.fori_loop(0, topk, n_body, (jnp.zeros_like(s1), jnp.zeros_like(a_mat)))
        z = jnp.sum(jnp.exp(a_mat - a0) * c2, axis=0, keepdims=True)
        e2_ref[:, cols] = (jnp.exp(s2 - b0) / z).astype(BF16)
        n1_ref[0, :, cols] = n1
        e1_ref[0, :, cols] = jnp.exp(s1 - a0)
        return 0

    lax.fori_loop(0, nt, gates, 0)


def _peer_main_kernel(ht_ref, u_ref, vt_ref, r2_ref, e2_ref, n1_ref, e1_ref, x_ref, mod_ref, g_ref,
                      out_ref, acc_ref, hs0_ref, hs1_ref, act0_ref, act1_ref, r2s_ref, e2s_ref, fence_ref,
                      *, n_heads, n_keys, eb, tt, n_blocks):
    k = pl.program_id(1)
    nsub = eb // n_keys
    pk = BF16_ROWS
    grp = n_keys // pk

    @pl.when(k == 0)
    def _():
        acc_ref[...] = jnp.zeros_like(acc_ref)
        hs1_ref[...] = jnp.zeros_like(hs1_ref)
        act0_ref[...] = jnp.zeros_like(act0_ref)
        act1_ref[...] = jnp.zeros_like(act1_ref)
        r2s_ref[...] = r2_ref[...]
        e2s_ref[...] = e2_ref[...]

    def gate_tiles(hs_ref, act_ref, blk, ii):
        i = blk * nsub + ii
        nrows, erows = n1_ref[i], e1_ref[i]
        rs = slice(ii * n_keys, (ii + 1) * n_keys)
        for tc in range(tt // LANES):
            cs_ = slice(tc * LANES, (tc + 1) * LANES)
            gsum = jnp.zeros((n_keys, LANES), BF16)
            for hd in range(n_heads):
                nb, ebc = (jnp.concatenate([jnp.broadcast_to(r[hd:hd + 1, cs_], (pk, LANES)).astype(BF16)] * grp, 0)
                           for r in (nrows, erows))
                ks = slice(hd * n_keys, (hd + 1) * n_keys)
                r2, e2 = r2s_ref[ks, cs_], e2s_ref[ks, cs_]
                gsum = gsum + jnp.where(r2 < nb, e2, jnp.zeros_like(e2)) * ebc
            act_ref[rs, cs_] = _gelu(hs_ref[rs, cs_]).astype(BF16) * gsum

    d = ht_ref.shape[0]
    npre, nmix = 4, 2
    kc = d // npre
    ec = eb // nmix

    def half_step(u_rows, hs_w, hs_r, act_w, act_r, blk):
        for ii in range(nsub):
            if ii % (nsub // npre) == 0:
                p = ii // (nsub // npre)
                u_part = pltpu.bitcast(u_ref[u_rows // 2:(u_rows + eb) // 2, p * kc:(p + 1) * kc], BF16)
                part = _dot(u_part, ht_ref[p * kc:(p + 1) * kc, :])
                if p == 0:
                    hs_w[...] = part
                else:
                    hs_w[...] += part
            gate_tiles(hs_r, act_w, blk, ii)
            if ii % (nsub // nmix) == 0:
                j = ii // (nsub // nmix)
                vt_part = pltpu.bitcast(vt_ref[:, u_rows + j * ec:u_rows + (j + 1) * ec], BF16)
                acc_ref[...] += _dot(vt_part, act_r[j * ec:(j + 1) * ec, :])

    half_step(0, hs0_ref, hs1_ref, act1_ref, act0_ref, jnp.maximum(2 * k - 1, 0))
    pl.semaphore_signal(fence_ref, 1)
    pl.semaphore_wait(fence_ref, 1)
    half_step(eb, hs1_ref, hs0_ref, act0_ref, act1_ref, jnp.minimum(2 * k, n_blocks - 1))

    @pl.when(k == pl.num_programs(1) - 1)
    def _():
        o = acc_ref[...].T
        out_ref[0] = x_ref[0] + mod_ref[0][5:6] * _rms(o, g_ref[...])


def _peer(x1, mod, g_pre, g_post, w_q, keys, exp_u, exp_v):
    b, l, d = x1.shape
    n_heads, _, n_keys, half = keys.shape
    n_exp = exp_u.shape[0]
    ntok = b * l
    tt = min(512, l)
    eb = 512
    tpb = l // tt
    wq_t = w_q.T.astype(BF16)
    keys_b = keys.astype(BF16)
    u_b = _pack_rows(exp_u, False)
    vt_b = _pack_rows(exp_v, True)
    tokb = lambda t, h: (t // tpb, t % tpb, 0)
    modb = lambda t, h: (t // tpb, 0, 0)
    hmap = lambda t, h: (h, 0, t)
    hshape = jax.ShapeDtypeStruct((n_heads, n_keys, ntok), F32)
    hshape_b = jax.ShapeDtypeStruct((n_heads * n_keys, ntok), BF16)
    ht, r2, e2, n1, e1 = pl.pallas_call(
        functools.partial(_peer_prep_kernel, n_keys=n_keys, topk=PEER_TOPK, tt=tt),
        grid=(ntok // tt, n_heads),
        in_specs=[pl.BlockSpec((1, tt, d), tokb), pl.BlockSpec((1,) + mod.shape[1:], modb),
                  pl.BlockSpec((1, d), lambda t, h: (0, 0)),
                  pl.BlockSpec((2 * half, d), lambda t, h: (h, 0)),
                  pl.BlockSpec((1, 2, n_keys, half), lambda t, h: (h, 0, 0, 0))],
        out_specs=[pl.BlockSpec((d, tt), lambda t, h: (0, t))] + [pl.BlockSpec((n_keys, tt), lambda t, h: (h, t))] * 2
                  + [pl.BlockSpec((1, n_keys, tt), hmap)] * 2,
        out_shape=[jax.ShapeDtypeStruct((d, ntok), BF16), hshape_b, hshape_b, hshape, hshape],
        scratch_shapes=[pltpu.VMEM((n_keys, tt), F32), pltpu.VMEM((n_keys, tt), F32),
                        pltpu.VMEM((tt // LANES, PEER_TOPK, LANES), F32), pltpu.VMEM((tt // LANES, PEER_TOPK, LANES), F32),
                        pltpu.VMEM((tt // LANES, 8, LANES), F32)],
        compiler_params=_cparams(("parallel", "arbitrary")),
        name="peer_prep",
    )(x1, mod, g_pre, wq_t, keys_b)

    n_blocks = n_exp // eb
    n_pairs = n_blocks // 2
    tokb2 = lambda t, k: (t // tpb, t % tpb, 0)
    hmap2 = lambda t, k: (0, 0, t)
    return pl.pallas_call(
        functools.partial(_peer_main_kernel, n_heads=n_heads, n_keys=n_keys, eb=eb, tt=tt, n_blocks=n_blocks),
        grid=(ntok // tt, n_pairs + 1),
        in_specs=[pl.BlockSpec((d, tt), lambda t, k: (0, t)),
                  pl.BlockSpec((eb, d), lambda t, k: (jnp.minimum(k, n_pairs - 1), 0)),
                  pl.BlockSpec((d // 2, 2 * eb), lambda t, k: (0, jnp.maximum(k - 1, 0)))]
                 + [pl.BlockSpec((n_heads * n_keys, tt), lambda t, k: (0, t))] * 2
                 + [pl.BlockSpec((n_keys, n_heads, tt), hmap2)] * 2
                 + [pl.BlockSpec((1, tt, d), tokb2),
                    pl.BlockSpec((1,) + mod.shape[1:], lambda t, k: (t // tpb, 0, 0)),
                    pl.BlockSpec((1, d), lambda t, k: (0, 0))],
        out_specs=pl.BlockSpec((1, tt, d), tokb2),
        out_shape=jax.ShapeDtypeStruct((b, l, d), F32),
        scratch_shapes=[pltpu.VMEM((d, tt), F32), pltpu.VMEM((eb, tt), F32), pltpu.VMEM((eb, tt), F32),
                        pltpu.VMEM((eb, tt), BF16), pltpu.VMEM((eb, tt), BF16),
                        pltpu.VMEM((n_heads * n_keys, tt), BF16), pltpu.VMEM((n_heads * n_keys, tt), BF16),
                        pltpu.SemaphoreType.REGULAR],
        compiler_params=_cparams(("parallel", "arbitrary")),
        name="peer_main",
    )(ht, u_b, vt_b, r2, e2, n1.transpose(1, 0, 2), e1.transpose(1, 0, 2), x1, mod, g_post)


def kernel(x, c, ctx, c_ctx, w_mod, b_mod, g_pre_mix, g_post_mix, g_pre_ffn, g_post_ffn, w_in, w_out,
           s5_lam_re, s5_lam_im, s5_log_dt, s5_b_re, s5_b_im, s5_c_re, s5_c_im, s5_d, s5_w_glu,
           gdn_conv, gdn_a_log, gdn_dt_bias, gdn_norm, peer_w_q, peer_keys, peer_u, peer_v):
    assert w_mod.shape[0] == 1, "single layer: the context stream only feeds the latent scans"
    b, l, d = x.shape
    lc = ctx.shape[1]
    n_heads = gdn_a_log.shape[-1]
    head_dim = gdn_norm.shape[-1]
    gdn_w = n_heads * head_dim
    s5_w = s5_d.shape[-1]
    n_gate = 2 * 2 * n_heads

    n_cond = -(-(b + 1) // 8) * 8
    cond = jnp.concatenate([c, c_ctx[None], jnp.zeros((n_cond - b - 1, d), F32)], axis=0)
    mod = _adaln(cond, w_mod[0], b_mod[0]).reshape(n_cond, 6, d)

    n_proj = w_in.shape[-1]
    w_pad = jnp.pad(w_in[0], ((0, 0), (0, -n_proj % LANES))).astype(BF16)
    g1 = g_pre_mix[0][None]
    ux, qkvx, zx, abx = _premix(x, mod[:b], True, g1, w_pad, s5_w, 3 * gdn_w, gdn_w)
    uc, qkvc, _, abc = _premix(ctx, mod[b:b + 1], False, g1, w_pad, s5_w, 3 * gdn_w, gdn_w)
    assert abx.shape[-1] == LANES and n_gate <= LANES

    mats = _s5_matrices(s5_lam_re[0], s5_lam_im[0], s5_log_dt[0], s5_b_re[0], s5_b_im[0],
                        s5_c_re[0], s5_c_im[0], s5_d[0])
    y_s5 = _s5(ux, uc, mats)

    qkvx = _gdn_conv(qkvx, gdn_conv[0], GRID_W, l // GRID_W > 1, n_heads, head_dim)
    qkvc = _gdn_conv(qkvc, gdn_conv[0], lc, False, n_heads, head_dim)
    o_gdn = _gdn(qkvx, abx, qkvc, abc, gdn_a_log[0], gdn_dt_bias[0], n_heads, head_dim)

    x1 = _postmix(x, y_s5, o_gdn, zx, mod[:b], gdn_norm[0][None], g_post_mix[0][None],
                  s5_w_glu[0].astype(BF16), w_out[0].astype(BF16), n_heads, head_dim)
    return _peer(x1, mod[:b], g_pre_ffn[0][None], g_post_ffn[0][None],
                 peer_w_q[0], peer_keys[0], peer_u[0], peer_v[0])
```

```python
import functools

import jax
import jax.numpy as jnp
from jax import lax
from jax.experimental import pallas as pl
from jax.experimental.pallas import tpu as pltpu

F32 = jnp.float32
BF16 = jnp.bfloat16
HI = lax.Precision.HIGHEST

EPS = 1e-6
GRID_W = 64
GDN_CHUNK = 64
S5_CHUNK = 8
PEER_TOPK = 16
LANES = 128
BF16_ROWS = 16
NEG = -1e30
VMEM_LIMIT = 56 * 1024 * 1024


def _cparams(sem):
    return pltpu.CompilerParams(dimension_semantics=sem, vmem_limit_bytes=VMEM_LIMIT)


def _sigmoid(x):
    return 1.0 / (1.0 + jnp.exp(-x))


def _softplus(x):
    return jnp.maximum(x, 0.0) + jnp.log(1.0 + jnp.exp(-jnp.abs(x)))


def _gelu(x):
    return 0.5 * x * (1.0 + lax.erf(x * (2.0 ** -0.5)))


def _rms(x, g):
    return x * lax.rsqrt(jnp.mean(x * x, axis=-1, keepdims=True) + EPS) * g


def _dot(a, b, dims=None, precision=None):
    if dims is None:
        dims = (((a.ndim - 1,), (0,)), ((), ()))
    return lax.dot_general(a, b, dims, precision=precision, preferred_element_type=F32)


def _pack_kernel(x_ref, o_ref, *, transpose):
    x = x_ref[...]
    x = x.T if transpose else x
    o_ref[...] = pltpu.bitcast(x.astype(BF16), jnp.uint32)


def _pack_rows(x, transpose):
    n, m = x.shape
    tn = 512
    if transpose:
        out_shape, out_spec = (m // 2, n), pl.BlockSpec((m // 2, tn), lambda i: (0, i))
    else:
        out_shape, out_spec = (n // 2, m), pl.BlockSpec((tn // 2, m), lambda i: (i, 0))
    return pl.pallas_call(
        functools.partial(_pack_kernel, transpose=transpose),
        grid=(n // tn,),
        in_specs=[pl.BlockSpec((tn, m), lambda i: (i, 0))],
        out_specs=out_spec,
        out_shape=jax.ShapeDtypeStruct(out_shape, jnp.uint32),
        compiler_params=_cparams(("parallel",)),
        name="pack_table",
    )(x)


_NT = (((1,), (1,)), ((), ()))
_TN = (((0,), (0,)), ((), ()))


def _adaln_kernel(c_ref, w_ref, b_ref, o_ref):
    c = c_ref[...]
    o_ref[...] = _dot(c * _sigmoid(c), w_ref[...], precision=HI) + b_ref[...]


def _adaln(cond, w_mod, b_mod):
    n, d = cond.shape
    m = w_mod.shape[1]
    tn = 512
    return pl.pallas_call(
        _adaln_kernel,
        grid=(m // tn,),
        in_specs=[pl.BlockSpec((n, d), lambda j: (0, 0)),
                  pl.BlockSpec((d, tn), lambda j: (0, j)),
                  pl.BlockSpec((1, tn), lambda j: (0, j))],
        out_specs=pl.BlockSpec((n, tn), lambda j: (0, j)),
        out_shape=jax.ShapeDtypeStruct((n, m), F32),
        compiler_params=_cparams(("arbitrary",)),
        name="adaln",
    )(cond, w_mod, b_mod.reshape(1, m))


def _premix_kernel(x_ref, mod_ref, g_ref, w_ref, u_ref, qkv_ref, z_ref, ab_ref, *, n_u, n_qkv, n_z):
    m = mod_ref[0]
    h = _rms(x_ref[0], g_ref[...]) * (1.0 + m[1:2]) + m[0:1]
    p = _dot(h.astype(BF16), w_ref[...])
    for j in range(u_ref.shape[0]):
        u_ref[j] = p[:, j * LANES:(j + 1) * LANES].reshape(u_ref.shape[1:])
    qkv_ref[0] = p[:, n_u:n_u + n_qkv]
    z_ref[0] = p[:, n_u + n_qkv:n_u + n_qkv + n_z].astype(BF16)
    ab_ref[0] = p[:, n_u + n_qkv + n_z:]


def _premix(x, mod, per_batch, g, w_pad, n_u, n_qkv, n_z):
    b, l, d = x.shape
    tm = min(512, l)
    tc = S5_CHUNK
    n_all = w_pad.shape[1]
    n_ab = n_all - n_u - n_qkv - n_z
    mod_map = (lambda i, t: (i, 0, 0)) if per_batch else (lambda i, t: (0, 0, 0))
    tok = lambda i, t: (i, t, 0)
    return pl.pallas_call(
        functools.partial(_premix_kernel, n_u=n_u, n_qkv=n_qkv, n_z=n_z),
        grid=(b, l // tm),
        in_specs=[pl.BlockSpec((1, tm, d), tok),
                  pl.BlockSpec((1,) + mod.shape[1:], mod_map),
                  pl.BlockSpec((1, d), lambda i, t: (0, 0)),
                  pl.BlockSpec((d, n_all), lambda i, t: (0, 0))],
        out_specs=[pl.BlockSpec((n_u // LANES, tm // tc, None, tc, LANES), lambda i, t: (0, t, i, 0, 0)),
                   pl.BlockSpec((1, tm, n_qkv), tok),
                   pl.BlockSpec((1, tm, n_z), tok), pl.BlockSpec((1, tm, n_ab), tok)],
        out_shape=[jax.ShapeDtypeStruct((n_u // LANES, l // tc, b, tc, LANES), F32), jax.ShapeDtypeStruct((b, l, n_qkv), F32),
                   jax.ShapeDtypeStruct((b, l, n_z), BF16), jax.ShapeDtypeStruct((b, l, n_ab), F32)],
        compiler_params=_cparams(("parallel", "parallel")),
        name="premix",
    )(x, mod, g, w_pad)


def _s5_matrices(lam_re, lam_im, log_dt, b_re, b_im, c_re, c_im, d_skip):
    t = S5_CHUNK
    n_dirs, g, p = lam_re.shape
    cg = b_re.shape[-1]
    dt = jnp.exp(log_dt)[..., None]
    lr, li = lam_re, lam_im
    den = lr * lr + li * li
    mag = jnp.exp(lr * dt)
    lb_re, lb_im = mag * jnp.cos(li * dt), mag * jnp.sin(li * dt)
    nr = lb_re - 1.0
    cr = (nr * lr + lb_im * li) / den
    ci = (lb_im * lr - nr * li) / den
    bb_re = cr[..., None] * b_re - ci[..., None] * b_im
    bb_im = cr[..., None] * b_im + ci[..., None] * b_re
    taus = jnp.arange(t + 1, dtype=F32)[:, None, None, None]
    pmag = jnp.exp(taus * (lr * dt))
    pw_re = pmag * jnp.cos(taus * (li * dt))
    pw_im = pmag * jnp.sin(taus * (li * dt))
    ein = functools.partial(jnp.einsum, precision=HI)
    lbr = pw_re[..., None] * bb_re - pw_im[..., None] * bb_im
    lbi = pw_re[..., None] * bb_im + pw_im[..., None] * bb_re
    kk = ein('dgcp,tdgpe->tdgce', c_re, lbr) - ein('dgcp,tdgpe->tdgce', c_im, lbi)
    er = c_re[None] * pw_re[:, :, :, None, :] - c_im[None] * pw_im[:, :, :, None, :]
    ei = c_re[None] * pw_im[:, :, :, None, :] + c_im[None] * pw_re[:, :, :, None, :]

    s_idx = jnp.arange(t)
    lag = s_idx[None, :] - s_idx[:, None]
    kf = kk[:, 0][jnp.clip(lag, 0, t)]
    kb = kk[:, 1][jnp.clip(-lag, 0, t)]
    m_full = (jnp.where((lag >= 0)[:, :, None, None, None], kf, 0.0)
              + jnp.where((lag <= 0)[:, :, None, None, None], kb, 0.0))
    eye_t = jnp.eye(t, dtype=F32)[:, :, None, None, None]
    eye_c = jnp.eye(cg, dtype=F32)[None, None, None]
    m_full = m_full + eye_t * eye_c * d_skip.reshape(g, cg)[None, None, :, :, None]
    p_parts = (lbr[:t, 0][::-1], lbi[:t, 0][::-1], lbr[:t, 1], lbi[:t, 1])
    r_parts = (er[1:, 0], -ei[1:, 0], er[1:, 1][::-1], -ei[1:, 1][::-1])

    gw = LANES // cg
    n8 = g // gw
    def spread(width):
        lane = jnp.arange(gw * width)
        tile = (lane[None, :] % width == jnp.arange(width)[:, None]).astype(F32)
        own = (lane[None, :] // width == jnp.arange(gw)[:, None]).astype(F32)
        return tile, own[None, None, :, None, None, :]

    tile_c, own_c = spread(cg)
    tile_p, own_p = spread(p)
    m6 = m_full.reshape(t, t, n8, gw, cg, cg).transpose(2, 0, 3, 5, 1, 4)
    mm = (ein('xsgctd,dl->xsgctl', m6, tile_c) * own_c).reshape(n8, t * LANES, t * LANES)
    p7 = jnp.stack([a.reshape(t, n8, gw, p, cg).transpose(1, 0, 2, 4, 3) for a in p_parts], axis=4)
    pq = (ein('xsgcqn,nl->xsgcql', p7, tile_p) * own_p).reshape(n8, t * LANES, 4 * gw * p)
    r7 = jnp.stack([a.reshape(t, n8, gw, cg, p).transpose(1, 2, 4, 0, 3) for a in r_parts], axis=1)
    rr = (ein('xqgntd,dl->xqgntl', r7, tile_c) * own_c).reshape(n8, 4 * gw * p, t * LANES)
    dec = jnp.stack([x.reshape(n8, gw * p) for x in (pw_re[t, 0], pw_im[t, 0], pw_re[t, 1], pw_im[t, 1])], axis=1)
    dec = jnp.concatenate([dec, jnp.zeros_like(dec)], axis=1)
    return pq.astype(BF16), mm.astype(BF16), rr.astype(BF16), dec


def _chunk_rows(ref, nb, t):
    n = ref.shape[0]
    return jnp.concatenate([ref[:, :, s, :].reshape(n * nb, LANES) for s in range(t)], axis=1).astype(BF16)


def _s5_state_kernel(x_ref, c_ref, pq_ref, dec_ref, sin_ref, loc_ref, locc_ref, *, nb, t, nct, ncx, ncc, sw):
    rt = pl.program_id(1)
    rows = nct * nb
    loc_ref[pl.ds(pl.multiple_of(rt * rows, rows), rows), :] = _dot(_chunk_rows(x_ref, nb, t), pq_ref[0])

    @pl.when(rt == 0)
    def _():
        locc_ref[...] = _dot(_chunk_rows(c_ref, nb, t), pq_ref[0])

    @pl.when(rt == pl.num_programs(1) - 1)
    def _():
        dec = dec_ref[0]
        far, fai, bar, bai = (jnp.broadcast_to(dec[i:i + 1], (nb, sw)) for i in range(4))

        def step(state, ar, ai, ref, k, c0):
            re, im = state
            r = pl.ds(pl.multiple_of(k * nb, nb), nb)
            return ar * re - ai * im + ref[r, c0:c0 + sw], ar * im + ai * re + ref[r, c0 + sw:c0 + 2 * sw]

        zero = jnp.zeros((nb, sw), F32)

        def ctx_body(i, carry):
            f, bk = carry
            return step(f, far, fai, locc_ref, i, 0), step(bk, bar, bai, locc_ref, ncc - 1 - i, 2 * sw)

        carry = lax.fori_loop(0, ncc, ctx_body, ((zero, zero), (zero, zero)))

        def pair_store(k0, c0, s_lo, s_hi):
            r = pl.ds(pl.multiple_of(k0 * nb, 2 * nb), 2 * nb)
            sin_ref[0, r, c0:c0 + sw] = jnp.concatenate([s_lo[0], s_hi[0]], axis=0).astype(BF16)
            sin_ref[0, r, c0 + sw:c0 + 2 * sw] = jnp.concatenate([s_lo[1], s_hi[1]], axis=0).astype(BF16)

        def x_body(j, carry):
            f0, b0 = carry
            f1 = step(f0, far, fai, loc_ref, 2 * j, 0)
            pair_store(2 * j, 0, f0, f1)
            kb = ncx - 2 - 2 * j
            b1 = step(b0, bar, bai, loc_ref, kb + 1, 2 * sw)
            pair_store(kb, 2 * sw, b1, b0)
            return step(f1, far, fai, loc_ref, 2 * j + 1, 0), step(b1, bar, bai, loc_ref, kb, 2 * sw)

        lax.fori_loop(0, ncx // 2, x_body, carry)


def _s5_out_kernel(x_ref, sin_ref, mm_ref, rr_ref, y_ref, *, nb, t):
    n = x_ref.shape[0]
    y = _dot(_chunk_rows(x_ref, nb, t), mm_ref[0]) + _dot(sin_ref[0], rr_ref[0])
    for s in range(t):
        y_ref[:, :, s, :] = y[:, s * LANES:(s + 1) * LANES].reshape(n, nb, LANES)


def _s5(ux, uc, mats):
    pq, mm, rr, dec = mats
    n8, ncx, nb, t, _ = ux.shape
    ncc = uc.shape[1]
    sw = dec.shape[-1]
    assert nb % 8 == 0 and t == S5_CHUNK and ncx % 2 == 0 and n8 == pq.shape[0]
    nct = min(64, ncx)
    ux5, uc5 = ux, uc
    kdim = t * LANES
    xblk = pl.BlockSpec((None, nct, nb, t, LANES), lambda g, r: (g, r, 0, 0, 0))
    wmap = lambda g, r: (g, 0, 0)
    sin = pl.pallas_call(
        functools.partial(_s5_state_kernel, nb=nb, t=t, nct=nct, ncx=ncx, ncc=ncc, sw=sw),
        grid=(n8, ncx // nct),
        in_specs=[xblk, pl.BlockSpec((None, ncc, nb, t, LANES), lambda g, r: (g, 0, 0, 0, 0)),
                  pl.BlockSpec((1, kdim, 4 * sw), wmap), pl.BlockSpec((1, 8, sw), wmap)],
        out_specs=pl.BlockSpec((1, ncx * nb, 4 * sw), wmap),
        out_shape=jax.ShapeDtypeStruct((n8, ncx * nb, 4 * sw), BF16),
        scratch_shapes=[pltpu.VMEM((ncx * nb, 4 * sw), F32), pltpu.VMEM((ncc * nb, 4 * sw), F32)],
        compiler_params=_cparams(("parallel", "arbitrary")),
        name="s5_state",
    )(ux5, uc5, pq, dec)
    y = pl.pallas_call(
        functools.partial(_s5_out_kernel, nb=nb, t=t),
        grid=(n8, ncx // nct),
        in_specs=[xblk, pl.BlockSpec((1, nct * nb, 4 * sw), lambda g, r: (g, r, 0)),
                  pl.BlockSpec((1, kdim, kdim), wmap), pl.BlockSpec((1, 4 * sw, kdim), wmap)],
        out_specs=xblk,
        out_shape=jax.ShapeDtypeStruct((n8, ncx, nb, t, LANES), F32),
        compiler_params=_cparams(("parallel", "parallel")),
        name="s5_out",
    )(ux5, sin, mm, rr)
    return y


_CONV_PAD = 72


def _gdn_conv_kernel(x_ref, w_ref, o_ref, pad_ref, *, length, cols, two_d, n_norm, n_q, q_scale):
    cb = pl.program_id(1)
    pad = _CONV_PAD
    zeros = jnp.zeros((pad, LANES), F32)
    pad_ref[0:pad, :] = zeros
    pad_ref[pad + length:pad + length + pad, :] = zeros
    pad_ref[pad:pad + length, :] = x_ref[0]
    rc = min(256, length)
    for r0 in range(0, length, rc):
        col = lax.rem(lax.broadcasted_iota(jnp.int32, (rc, LANES), 0) + r0, cols)
        acc = jnp.zeros((rc, LANES), F32)
        for dr in ((0, 1, 2) if two_d else (1,)):
            for dc in range(3):
                off = (dr - 1) * cols + (dc - 1)
                term = pad_ref[pad + r0 + off:pad + r0 + off + rc, :] * w_ref[dr * 3 + dc:dr * 3 + dc + 1, :]
                if dc == 0:
                    term = jnp.where(col >= 1, term, 0.0)
                if dc == 2:
                    term = jnp.where(col <= cols - 2, term, 0.0)
                acc = acc + term
        y = acc * _sigmoid(acc)
        nrm = y * lax.rsqrt(jnp.sum(y * y, axis=-1, keepdims=True) + EPS)
        nrm = nrm * jnp.where(cb < n_q, q_scale, 1.0)
        o_ref[0, r0:r0 + rc, :] = jnp.where(cb < n_norm, nrm, y)


def _gdn_conv(qkv, conv_w, cols, two_d, n_heads, head_dim):
    b, l, ch = qkv.shape
    w9 = jnp.concatenate([conv_w.reshape(9, ch), jnp.zeros((7, ch), F32)], axis=0)
    blk = lambda i, j: (i, 0, j)
    return pl.pallas_call(
        functools.partial(_gdn_conv_kernel, length=l, cols=cols, two_d=two_d,
                          n_norm=2 * n_heads, n_q=n_heads, q_scale=head_dim ** -0.5),
        grid=(b, ch // LANES),
        in_specs=[pl.BlockSpec((1, l, LANES), blk), pl.BlockSpec((16, LANES), lambda i, j: (0, j))],
        out_specs=pl.BlockSpec((1, l, LANES), blk),
        out_shape=jax.ShapeDtypeStruct((b, l, ch), F32),
        scratch_shapes=[pltpu.VMEM((l + 2 * _CONV_PAD, LANES), F32)],
        compiler_params=_cparams(("parallel", "parallel")),
        name="gdn_conv",
    )(qkv, w9)


def _gdn_kernel(qx_ref, kx_ref, vx_ref, abx_ref, qc_ref, kc_ref, vc_ref, abc_ref,
                pl_ref, o_ref, gcx_ref, bcx_ref, gcc_ref, bcc_ref,
                u_ref, w_ref, qd_ref, kd_ref, at_ref, gt_ref, *, n_heads, ncx, ncc):
    cs = GDN_CHUNK
    hd = qx_ref.shape[-1]
    head = pl.program_id(1)
    alog_l, dtb_l = pl_ref[0:1, :], pl_ref[1:2, :]
    lane = lax.broadcasted_iota(jnp.int32, (1, LANES), 1)

    def gate_columns(ab_ref, g_out, b_out, n_rows):
        rc = min(256, n_rows)
        for r0 in range(0, n_rows, rc):
            ab = ab_ref[0, r0:r0 + rc, :]
            g_all = -jnp.exp(alog_l) * _softplus(ab + dtb_l)
            b_all = _sigmoid(ab)
            for dr in range(2):
                j = dr * n_heads + head
                g = jnp.sum(jnp.where(lane == j, g_all, 0.0), axis=-1, keepdims=True)
                bt = jnp.sum(jnp.where(lane == j + 2 * n_heads, b_all, 0.0), axis=-1, keepdims=True)
                g_out[dr, r0:r0 + rc, :] = jnp.broadcast_to(g, (rc, hd))
                b_out[dr, r0:r0 + rc, :] = jnp.broadcast_to(bt, (rc, hd))

    gate_columns(abx_ref, gcx_ref, bcx_ref, ncx * cs)
    gate_columns(abc_ref, gcc_ref, bcc_ref, ncc * cs)

    nblk = 4
    side = nblk * cs
    def block_masks():
        ri = lax.broadcasted_iota(jnp.int32, (side, side), 0)
        ci = lax.broadcasted_iota(jnp.int32, (side, side), 1)
        dist = jnp.where(ri < side // 2, ri - ci, ci - ri)
        dist = jnp.where((ri // cs) == (ci // cs), dist, -1)
        return dist >= 0, dist > 0, jnp.where(ri == ci, 1.0, 0.0)

    def cumsum_blocks(x):
        row_id = lax.broadcasted_iota(jnp.int32, (side, hd), 0)
        pos = row_id % cs
        is_fwd = row_id < side // 2
        s = 1
        while s < cs:
            x = x + jnp.where(is_fwd, jnp.where(pos >= s, pltpu.roll(x, s, 0), 0.0),
                              jnp.where(pos < cs - s, pltpu.roll(x, side - s, 0), 0.0))
            s *= 2
        return x

    def each(f, *lists):
        return [f(*args) for args in zip(*lists)]

    def chunks_local(c0s, refs, base):
        q_ref, k_ref, v_ref, gcol_ref, bcol_ref = refs
        incl, strict, eye = block_masks()
        rows = [pl.ds(pl.multiple_of(c0 * cs, 2 * cs), 2 * cs) for c0 in c0s]
        q, k, v = ([jnp.concatenate([r[0, rw, :]] * 2, axis=0) for rw in rows] for r in (q_ref, k_ref, v_ref))
        gb, bb = ([jnp.concatenate([r[0, rw, :], r[1, rw, :]], axis=0) for rw in rows] for r in (gcol_ref, bcol_ref))
        gcs = each(cumsum_blocks, gb)
        gwide = each(lambda g: jnp.concatenate([g] * (side // hd), axis=1), gcs)
        decay = each(lambda g: jnp.where(incl, jnp.exp(jnp.where(incl, g - g.T[0:1, :], 0.0)), 0.0), gwide)
        kbf = each(lambda t: t.astype(BF16), k)
        kb = each(lambda t, b_: t * b_, k, bb)
        lmat = each(lambda x, y, dc: jnp.where(strict, _dot(x.astype(BF16), y, _NT) * dc, 0.0), kb, kbf, decay)
        pw = each(lambda l_: -l_, lmat)
        tmat = each(lambda p_: eye + p_, pw)
        for _ in range(cs.bit_length() - 2):
            pw = each(lambda p_: _dot(p_.astype(BF16), p_.astype(BF16)), pw)
            tmat = each(lambda t_, p_: t_ + _dot(t_.astype(BF16), p_.astype(BF16)), tmat, pw)
        g_exp = each(jnp.exp, gcs)
        rhs = each(lambda v_, b_, kb_, ge: jnp.concatenate([v_ * b_, kb_ * ge], axis=1).astype(BF16), v, bb, kb, g_exp)
        uw = each(lambda t_, r_: _dot(t_.astype(BF16), r_), tmat, rhs)
        attn = each(lambda q_, y, dc: jnp.where(incl, _dot(q_.astype(BF16), y, _NT) * dc, 0.0), q, kbf, decay)
        half = side // 2
        for c0, uw_, attn_, q_, k_, g_, ge in zip(c0s, uw, attn, q, k, gcs, g_exp):
            lasts = [g_[(j + 1) * cs - 1:(j + 1) * cs, :] if j < nblk // 2 else g_[j * cs:j * cs + 1, :] for j in range(nblk)]
            g_last = jnp.concatenate([jnp.broadcast_to(r, (cs, hd)) for r in lasts], axis=0)
            qd = (q_ * ge).astype(BF16)
            kd = (k_ * jnp.exp(g_last - g_)).astype(BF16)
            srows = pl.ds(pl.multiple_of((base + c0) * cs, 2 * cs), 2 * cs)
            grows = pl.ds(pl.multiple_of((base + c0) * 8, 16), 16)
            for dr in range(2):
                blk = slice(dr * half, (dr + 1) * half)
                u_ref[dr, srows, :] = uw_[blk, :hd]
                w_ref[dr, srows, :] = uw_[blk, hd:].astype(BF16)
                qd_ref[dr, srows, :] = qd[blk]
                kd_ref[dr, srows, :] = kd[blk]
                at_ref[dr, srows, :] = jnp.concatenate(
                    [attn_[j * cs:(j + 1) * cs, j * cs:(j + 1) * cs] for j in (2 * dr, 2 * dr + 1)], axis=0).astype(BF16)
                gt_ref[dr, grows, :] = jnp.concatenate(
                    [jnp.broadcast_to(jnp.exp(lasts[j]), (8, hd)) for j in (2 * dr, 2 * dr + 1)], axis=0)

    def chunks_seq(cgs, states):
        drs = (0, 1)
        srows = [pl.ds(pl.multiple_of(cg * cs, cs), cs) for cg in cgs]
        sb = [st.astype(BF16) for st in states]
        vb = [(u_ref[dr, r, :] - _dot(w_ref[dr, r, :], s_)).astype(BF16) for dr, r, s_ in zip(drs, srows, sb)]
        new = [st * gt_ref[dr, pl.ds(pl.multiple_of(cg * 8, 8), 8), :][0:1] + _dot(kd_ref[dr, r, :], v_, _TN)
               for dr, cg, r, st, v_ in zip(drs, cgs, srows, states, vb)]
        outs = [_dot(qd_ref[dr, r, :], s_) + _dot(at_ref[dr, r, :], v_) for dr, r, s_, v_ in zip(drs, srows, sb, vb)]
        return outs, new

    refs_c = (qc_ref, kc_ref, vc_ref, gcc_ref, bcc_ref)
    refs_x = (qx_ref, kx_ref, vx_ref, gcx_ref, bcx_ref)

    def local_pass(n, refs, base):
        per = 8 if n % 8 == 0 else 4

        def body(i, _):
            chunks_local([per * i + 2 * j for j in range(per // 2)], refs, base)
            return 0

        lax.fori_loop(0, n // per, body, 0)

    local_pass(ncc, refs_c, 0)
    local_pass(ncx, refs_x, ncc)

    zero = jnp.zeros((hd, hd), F32)

    def ctx_body(i, carry):
        return tuple(chunks_seq((i, ncc - 1 - i), carry)[1])

    carry = lax.fori_loop(0, ncc, ctx_body, (zero, zero))
    o_ref[...] = jnp.zeros_like(o_ref)

    def x_body(i, carry):
        ib = ncx - 1 - i
        (of, ob), new = chunks_seq((ncc + i, ncc + ib), carry)
        o_ref[0, pl.ds(pl.multiple_of(i * cs, cs), cs), :] += of
        o_ref[0, pl.ds(pl.multiple_of(ib * cs, cs), cs), :] += ob
        return tuple(new)

    lax.fori_loop(0, ncx, x_body, carry)


def _gdn(qkvx, abx, qkvc, abc, a_log, dt_bias, n_heads, head_dim):
    b, l, _ = qkvx.shape
    lc = qkvc.shape[1]
    cs = GDN_CHUNK
    ncx, ncc = l // cs, lc // cs
    assert ncx % 4 == 0 and ncc % 4 == 0, "chunks are processed four at a time"
    ng = 2 * n_heads
    p_l = jnp.zeros((8, LANES), F32)
    p_l = p_l.at[0, :ng].set(a_log.reshape(ng)).at[1, :ng].set(dt_bias.reshape(ng))

    def col(off):
        return lambda i, h: (i, 0, off + h)

    full = lambda i, h: (i, 0, 0)
    const = lambda i, h: (0, 0)
    hb = head_dim
    return pl.pallas_call(
        functools.partial(_gdn_kernel, n_heads=n_heads, ncx=ncx, ncc=ncc),
        grid=(b, n_heads),
        in_specs=[pl.BlockSpec((1, l, hb), col(0)), pl.BlockSpec((1, l, hb), col(n_heads)),
                  pl.BlockSpec((1, l, hb), col(2 * n_heads)),
                  pl.BlockSpec((1, l, LANES), full),
                  pl.BlockSpec((1, lc, hb), col(0)), pl.BlockSpec((1, lc, hb), col(n_heads)),
                  pl.BlockSpec((1, lc, hb), col(2 * n_heads)),
                  pl.BlockSpec((1, lc, LANES), full),
                  pl.BlockSpec((8, LANES), const)],
        out_specs=pl.BlockSpec((1, l, hb), col(0)),
        out_shape=jax.ShapeDtypeStruct((b, l, n_heads * head_dim), F32),
        scratch_shapes=[pltpu.VMEM((2, l, hb), F32), pltpu.VMEM((2, l, hb), F32),
                        pltpu.VMEM((2, lc, hb), F32), pltpu.VMEM((2, lc, hb), F32),
                        pltpu.VMEM((2, l + lc, hb), F32), pltpu.VMEM((2, l + lc, hb), BF16),
                        pltpu.VMEM((2, l + lc, hb), BF16), pltpu.VMEM((2, l + lc, hb), BF16),
                        pltpu.VMEM((2, l + lc, cs), BF16), pltpu.VMEM((2, (ncx + ncc) * 8, hb), F32)],
        compiler_params=_cparams(("parallel", "arbitrary")),
        name="gdn",
    )(qkvx, qkvx, qkvx, abx, qkvc, qkvc, qkvc, abc, p_l)


def _postmix_kernel(x_ref, y_ref, o_ref, z_ref, mod_ref, gn_ref, gpost_ref, wglu_ref, wout_ref, out_ref,
                    *, n_heads, head_dim, s5_w):
    s = _gelu(jnp.concatenate([y_ref[j].reshape(x_ref.shape[1], LANES) for j in range(y_ref.shape[0])], axis=1))
    s = s * _sigmoid(_dot(s.astype(BF16), wglu_ref[...]))
    o = o_ref[0]
    z = z_ref[0].astype(F32)
    parts = []
    for h in range(n_heads):
        sl = slice(h * head_dim, (h + 1) * head_dim)
        zh = z[:, sl]
        parts.append(_rms(o[:, sl], gn_ref[...]) * (zh * _sigmoid(zh)))
    gd = jnp.concatenate(parts, axis=-1)
    mixed = _dot(s.astype(BF16), wout_ref[0:s5_w, :]) + _dot(gd.astype(BF16), wout_ref[s5_w:, :])
    gate = mod_ref[0][2:3]
    out_ref[0] = x_ref[0] + gate * _rms(mixed, gpost_ref[...])


def _postmix(x, y, o, z, mod, gdn_norm, g_post, w_glu, w_out, n_heads, head_dim):
    b, l, d = x.shape
    s5_w = y.shape[0] * y.shape[-1]
    gw = o.shape[-1]
    tm = min(512, l)
    tc = S5_CHUNK
    tok = lambda i, t: (i, t, 0)
    const = lambda i, t: (0, 0)
    return pl.pallas_call(
        functools.partial(_postmix_kernel, n_heads=n_heads, head_dim=head_dim, s5_w=s5_w),
        grid=(b, l // tm),
        in_specs=[pl.BlockSpec((1, tm, d), tok), pl.BlockSpec((s5_w // LANES, tm // tc, None, tc, LANES), lambda i, t: (0, t, i, 0, 0)),
                  pl.BlockSpec((1, tm, gw), tok), pl.BlockSpec((1, tm, gw), tok),
                  pl.BlockSpec((1,) + mod.shape[1:], lambda i, t: (i, 0, 0)),
                  pl.BlockSpec((1, head_dim), const), pl.BlockSpec((1, d), const),
                  pl.BlockSpec((s5_w, s5_w), const), pl.BlockSpec((s5_w + gw, d), const)],
        out_specs=pl.BlockSpec((1, tm, d), tok),
        out_shape=jax.ShapeDtypeStruct((b, l, d), F32),
        compiler_params=_cparams(("parallel", "parallel")),
        name="postmix",
    )(x, y, o, z, mod, gdn_norm, g_post, w_glu, w_out)


def _peer_prep_kernel(x_ref, mod_ref, g_ref, wq_ref, keys_ref, ht_ref, r2_ref, e2_ref, n1_ref, e1_ref,
                      s1_ref, s2_ref, lv1_ref, lv2_ref, tau_ref, *, n_keys, topk, tt):
    @pl.when(pl.program_id(1) == 0)
    def _():
        m = mod_ref[0]
        h = _rms(x_ref[0], g_ref[...]) * (1.0 + m[4:5]) + m[3:4]
        ht_ref[...] = h.T.astype(BF16)

    half = keys_ref.shape[-1]
    ht = ht_ref[...]
    q1 = _dot(wq_ref[0:half, :], ht).astype(BF16)
    q2 = _dot(wq_ref[half:2 * half, :], ht).astype(BF16)
    s1_ref[...] = _dot(keys_ref[0, 0], q1)
    s2_ref[...] = _dot(keys_ref[0, 1], q2)
    kf = float(topk)

    nt = tt // LANES

    def lane_tile(t):
        return pl.ds(pl.multiple_of(t * LANES, LANES), LANES)

    def levels(t, _):
        cols = lane_tile(t)

        def lv_body(q, carry):
            c1, c2, rank2 = carry
            m1 = jnp.max(c1, axis=0, keepdims=True)
            m2 = jnp.max(c2, axis=0, keepdims=True)
            lv1_ref[t, pl.ds(q, 1), :] = m1
            lv2_ref[t, pl.ds(q, 1), :] = m2
            h1 = c1 == jnp.where(m1 > NEG, m1, -NEG)
            h2 = c2 == jnp.where(m2 > NEG, m2, -NEG)
            rank2 = jnp.where(h2, jnp.asarray(q, F32), rank2)
            return jnp.where(h1, NEG, c1), jnp.where(h2, NEG, c2), rank2

        s1, s2 = s1_ref[:, cols], s2_ref[:, cols]
        _, _, rank2 = lax.fori_loop(0, topk, lv_body, (s1, s2, jnp.full(s2.shape, kf, F32)))
        r2_ref[:, cols] = rank2.astype(BF16)
        return 0

    lax.fori_loop(0, nt, levels, 0)

    a_all, b_all = (jnp.concatenate([r[j] for j in range(nt)], axis=1) for r in (lv1_ref, lv2_ref))
    sub = topk // 2
    cand = jnp.concatenate([a_all[0:1] + b_all] + [a_all[p:p + 1] + b_all[0:sub] for p in range(1, sub)]
                           + [a_all[sub:] + b_all[0:1]], axis=0)

    def tau_body(_, carry):
        cand, cnt, tau = carry
        mx = jnp.max(cand, axis=0, keepdims=True)
        hit = cand == mx
        tau = jnp.where(cnt < kf, mx, tau)
        cnt = cnt + jnp.sum(jnp.where(hit, 1.0, 0.0), axis=0, keepdims=True)
        return jnp.where(hit, NEG, cand), cnt, tau

    row0 = a_all[0:1]
    _, _, tau_all = lax.fori_loop(0, topk, tau_body, (cand, jnp.zeros_like(row0), jnp.full_like(row0, NEG)))
    for j in range(nt):
        tau_ref[j] = jnp.broadcast_to(tau_all[:, j * LANES:(j + 1) * LANES], tau_ref.shape[1:])

    def gates(t, _):
        cols = lane_tile(t)
        s1, s2 = s1_ref[:, cols], s2_ref[:, cols]
        a_mat, b_mat, tau = lv1_ref[t], lv2_ref[t], tau_ref[t][0:1]
        a0, b0 = a_mat[0:1], b_mat[0:1]

        def n_body(q, carry):
            n1, c2 = carry
            b = lv2_ref[t, pl.ds(q, 1), :]
            n1 = n1 + jnp.where(s1 + b >= tau, 1.0, 0.0)
            c2 = c2 + jnp.where(a_mat + b >= tau, jnp.exp(b - b0), 0.0)
            return n1, c2

        n1, c2 = lax.fori_loop(0, topk, n_body, (jnp.zeros_like(s1), jnp.zeros_like(a_mat)))
        z = jnp.sum(jnp.exp(a_mat - a0) * c2, axis=0, keepdims=True)
        e2_ref[:, cols] = (jnp.exp(s2 - b0) / z).astype(BF16)
        n1_ref[0, :, cols] = n1
        e1_ref[0, :, cols] = jnp.exp(s1 - a0)
        return 0

    lax.fori_loop(0, nt, gates, 0)


def _peer_main_kernel(ht_ref, u_ref, vt_ref, r2_ref, e2_ref, n1_ref, e1_ref, x_ref, mod_ref, g_ref,
                      out_ref, acc_ref, hs0_ref, hs1_ref, act0_ref, act1_ref, r2s_ref, e2s_ref, fence_ref,
                      *, n_heads, n_keys, eb, tt, n_blocks):
    k = pl.program_id(1)
    nsub = eb // n_keys
    pk = BF16_ROWS
    grp = n_keys // pk

    last = pl.num_programs(1) - 1

    @pl.when(k == 0)
    def _():
        acc_ref[...] = jnp.zeros_like(acc_ref)
        r2s_ref[...] = r2_ref[...]
        e2s_ref[...] = e2_ref[...]

    def gate_tiles(hs_ref, act_ref, blk, ii):
        i = blk * nsub + ii
        nrows, erows = n1_ref[i], e1_ref[i]
        rs = slice(ii * n_keys, (ii + 1) * n_keys)
        for tc in range(tt // LANES):
            cs_ = slice(tc * LANES, (tc + 1) * LANES)
            gsum = jnp.zeros((n_keys, LANES), BF16)
            for hd in range(n_heads):
                nb, ebc = (jnp.concatenate([jnp.broadcast_to(r[hd:hd + 1, cs_], (pk, LANES)).astype(BF16)] * grp, 0)
                           for r in (nrows, erows))
                ks = slice(hd * n_keys, (hd + 1) * n_keys)
                r2, e2 = r2s_ref[ks, cs_], e2s_ref[ks, cs_]
                gsum = gsum + jnp.where(r2 < nb, e2, jnp.zeros_like(e2)) * ebc
            act_ref[rs, cs_] = _gelu(hs_ref[rs, cs_]).astype(BF16) * gsum

    d = ht_ref.shape[0]
    npre, nmix = 4, 2
    kc = d // npre
    ec = eb // nmix

    def half_step(u_rows, hs_w, hs_r, act_w, act_r, blk, pre=True, gate=True, mix=True):
        for ii in range(nsub):
            if pre and ii % (nsub // npre) == 0:
                p = ii // (nsub // npre)
                u_part = pltpu.bitcast(u_ref[u_rows // 2:(u_rows + eb) // 2, p * kc:(p + 1) * kc], BF16)
                part = _dot(u_part, ht_ref[p * kc:(p + 1) * kc, :])
                if p == 0:
                    hs_w[...] = part
                else:
                    hs_w[...] += part
            if gate:
                gate_tiles(hs_r, act_w, blk, ii)
            if mix and ii % (nsub // nmix) == 0:
                j = ii // (nsub // nmix)
                vt_part = pltpu.bitcast(vt_ref[:, u_rows + j * ec:u_rows + (j + 1) * ec], BF16)
                acc_ref[...] += _dot(vt_part, act_r[j * ec:(j + 1) * ec, :])

    def step(first, final):
        half_step(0, hs0_ref, hs1_ref, act1_ref, act0_ref, 2 * k - 1, pre=not final, gate=not first, mix=not first)
        pl.semaphore_signal(fence_ref, 1)
        pl.semaphore_wait(fence_ref, 1)
        half_step(eb, hs1_ref, hs0_ref, act0_ref, act1_ref, 2 * k, pre=not final, gate=not final, mix=not first)

    @pl.when(k == 0)
    def _():
        step(True, False)

    @pl.when((k > 0) & (k < last))
    def _():
        step(False, False)

    @pl.when(k == last)
    def _():
        step(False, True)
        o = acc_ref[...].T
        out_ref[0] = x_ref[0] + mod_ref[0][5:6] * _rms(o, g_ref[...])


def _peer(x1, mod, g_pre, g_post, w_q, keys, exp_u, exp_v):
    b, l, d = x1.shape
    n_heads, _, n_keys, half = keys.shape
    n_exp = exp_u.shape[0]
    ntok = b * l
    tt = min(512, l)
    eb = 512
    tpb = l // tt
    wq_t = w_q.T.astype(BF16)
    keys_b = keys.astype(BF16)
    u_b = _pack_rows(exp_u, False)
    vt_b = _pack_rows(exp_v, True)
    tokb = lambda t, h: (t // tpb, t % tpb, 0)
    modb = lambda t, h: (t // tpb, 0, 0)
    hmap = lambda t, h: (h, 0, t)
    hshape = jax.ShapeDtypeStruct((n_heads, n_keys, ntok), F32)
    hshape_b = jax.ShapeDtypeStruct((n_heads * n_keys, ntok), BF16)
    ht, r2, e2, n1, e1 = pl.pallas_call(
        functools.partial(_peer_prep_kernel, n_keys=n_keys, topk=PEER_TOPK, tt=tt),
        grid=(ntok // tt, n_heads),
        in_specs=[pl.BlockSpec((1, tt, d), tokb), pl.BlockSpec((1,) + mod.shape[1:], modb),
                  pl.BlockSpec((1, d), lambda t, h: (0, 0)),
                  pl.BlockSpec((2 * half, d), lambda t, h: (h, 0)),
                  pl.BlockSpec((1, 2, n_keys, half), lambda t, h: (h, 0, 0, 0))],
        out_specs=[pl.BlockSpec((d, tt), lambda t, h: (0, t))] + [pl.BlockSpec((n_keys, tt), lambda t, h: (h, t))] * 2
                  + [pl.BlockSpec((1, n_keys, tt), hmap)] * 2,
        out_shape=[jax.ShapeDtypeStruct((d, ntok), BF16), hshape_b, hshape_b, hshape, hshape],
        scratch_shapes=[pltpu.VMEM((n_keys, tt), F32), pltpu.VMEM((n_keys, tt), F32),
                        pltpu.VMEM((tt // LANES, PEER_TOPK, LANES), F32), pltpu.VMEM((tt // LANES, PEER_TOPK, LANES), F32),
                        pltpu.VMEM((tt // LANES, 8, LANES), F32)],
        compiler_params=_cparams(("parallel", "arbitrary")),
        name="peer_prep",
    )(x1, mod, g_pre, wq_t, keys_b)

    n_blocks = n_exp // eb
    n_pairs = n_blocks // 2
    tokb2 = lambda t, k: (t // tpb, t % tpb, 0)
    hmap2 = lambda t, k: (0, 0, t)
    return pl.pallas_call(
        functools.partial(_peer_main_kernel, n_heads=n_heads, n_keys=n_keys, eb=eb, tt=tt, n_blocks=n_blocks),
        grid=(ntok // tt, n_pairs + 1),
        in_specs=[pl.BlockSpec((d, tt), lambda t, k: (0, t)),
                  pl.BlockSpec((eb, d), lambda t, k: (jnp.minimum(k, n_pairs - 1), 0)),
                  pl.BlockSpec((d // 2, 2 * eb), lambda t, k: (0, jnp.maximum(k - 1, 0)))]
                 + [pl.BlockSpec((n_heads * n_keys, tt), lambda t, k: (0, t))] * 2
                 + [pl.BlockSpec((n_keys, n_heads, tt), hmap2)] * 2
                 + [pl.BlockSpec((1, tt, d), tokb2),
                    pl.BlockSpec((1,) + mod.shape[1:], lambda t, k: (t // tpb, 0, 0)),
                    pl.BlockSpec((1, d), lambda t, k: (0, 0))],
        out_specs=pl.BlockSpec((1, tt, d), tokb2),
        out_shape=jax.ShapeDtypeStruct((b, l, d), F32),
        scratch_shapes=[pltpu.VMEM((d, tt), F32), pltpu.VMEM((eb, tt), F32), pltpu.VMEM((eb, tt), F32),
                        pltpu.VMEM((eb, tt), BF16), pltpu.VMEM((eb, tt), BF16),
                        pltpu.VMEM((n_heads * n_keys, tt), BF16), pltpu.VMEM((n_heads * n_keys, tt), BF16),
                        pltpu.SemaphoreType.REGULAR],
        compiler_params=_cparams(("parallel", "arbitrary")),
        name="peer_main",
    )(ht, u_b, vt_b, r2, e2, n1.transpose(1, 0, 2), e1.transpose(1, 0, 2), x1, mod, g_post)


def kernel(x, c, ctx, c_ctx, w_mod, b_mod, g_pre_mix, g_post_mix, g_pre_ffn, g_post_ffn, w_in, w_out,
           s5_lam_re, s5_lam_im, s5_log_dt, s5_b_re, s5_b_im, s5_c_re, s5_c_im, s5_d, s5_w_glu,
           gdn_conv, gdn_a_log, gdn_dt_bias, gdn_norm, peer_w_q, peer_keys, peer_u, peer_v):
    assert w_mod.shape[0] == 1, "single layer: the context stream only feeds the latent scans"
    b, l, d = x.shape
    lc = ctx.shape[1]
    n_heads = gdn_a_log.shape[-1]
    head_dim = gdn_norm.shape[-1]
    gdn_w = n_heads * head_dim
    s5_w = s5_d.shape[-1]
    n_gate = 2 * 2 * n_heads

    n_cond = -(-(b + 1) // 8) * 8
    cond = jnp.concatenate([c, c_ctx[None], jnp.zeros((n_cond - b - 1, d), F32)], axis=0)
    mod = _adaln(cond, w_mod[0], b_mod[0]).reshape(n_cond, 6, d)

    n_proj = w_in.shape[-1]
    w_pad = jnp.pad(w_in[0], ((0, 0), (0, -n_proj % LANES))).astype(BF16)
    g1 = g_pre_mix[0][None]
    ux, qkvx, zx, abx = _premix(x, mod[:b], True, g1, w_pad, s5_w, 3 * gdn_w, gdn_w)
    uc, qkvc, _, abc = _premix(ctx, mod[b:b + 1], False, g1, w_pad, s5_w, 3 * gdn_w, gdn_w)
    assert abx.shape[-1] == LANES and n_gate <= LANES

    mats = _s5_matrices(s5_lam_re[0], s5_lam_im[0], s5_log_dt[0], s5_b_re[0], s5_b_im[0],
                        s5_c_re[0], s5_c_im[0], s5_d[0])
    y_s5 = _s5(ux, uc, mats)

    qkvx = _gdn_conv(qkvx, gdn_conv[0], GRID_W, l // GRID_W > 1, n_heads, head_dim)
    qkvc = _gdn_conv(qkvc, gdn_conv[0], lc, False, n_heads, head_dim)
    o_gdn = _gdn(qkvx, abx, qkvc, abc, gdn_a_log[0], gdn_dt_bias[0], n_heads, head_dim)

    x1 = _postmix(x, y_s5, o_gdn, zx, mod[:b], gdn_norm[0][None], g_post_mix[0][None],
                  s5_w_glu[0].astype(BF16), w_out[0].astype(BF16), n_heads, head_dim)
    return _peer(x1, mod[:b], g_pre_ffn[0][None], g_post_ffn[0][None],
                 peer_w_q[0], peer_keys[0], peer_u[0], peer_v[0])
```

```python
import functools

import jax
import jax.numpy as jnp
from jax import lax
from jax.experimental import pallas as pl
from jax.experimental.pallas import tpu as pltpu

F32 = jnp.float32
BF16 = jnp.bfloat16
HI = lax.Precision.HIGHEST

EPS = 1e-6
GRID_W = 64
GDN_CHUNK = 64
S5_CHUNK = 8
PEER_TOPK = 16
LANES = 128
BF16_ROWS = 16
NEG = -1e30
VMEM_LIMIT = 56 * 1024 * 1024


def _cparams(sem):
    return pltpu.CompilerParams(dimension_semantics=sem, vmem_limit_bytes=VMEM_LIMIT)


def _sigmoid(x):
    return 1.0 / (1.0 + jnp.exp(-x))


def _softplus(x):
    return jnp.maximum(x, 0.0) + jnp.log(1.0 + jnp.exp(-jnp.abs(x)))


def _gelu(x):
    return 0.5 * x * (1.0 + lax.erf(x * (2.0 ** -0.5)))


def _rms(x, g):
    return x * lax.rsqrt(jnp.mean(x * x, axis=-1, keepdims=True) + EPS) * g


def _dot(a, b, dims=None, precision=None):
    if dims is None:
        dims = (((a.ndim - 1,), (0,)), ((), ()))
    return lax.dot_general(a, b, dims, precision=precision, preferred_element_type=F32)


def _pack_kernel(x_ref, o_ref, *, transpose):
    x = x_ref[...]
    x = x.T if transpose else x
    o_ref[...] = pltpu.bitcast(x.astype(BF16), jnp.uint32)


def _pack_rows(x, transpose):
    n, m = x.shape
    tn = 512
    if transpose:
        out_shape, out_spec = (m // 2, n), pl.BlockSpec((m // 2, tn), lambda i: (0, i))
    else:
        out_shape, out_spec = (n // 2, m), pl.BlockSpec((tn // 2, m), lambda i: (i, 0))
    return pl.pallas_call(
        functools.partial(_pack_kernel, transpose=transpose),
        grid=(n // tn,),
        in_specs=[pl.BlockSpec((tn, m), lambda i: (i, 0))],
        out_specs=out_spec,
        out_shape=jax.ShapeDtypeStruct(out_shape, jnp.uint32),
        compiler_params=_cparams(("parallel",)),
        name="pack_table",
    )(x)


_NT = (((1,), (1,)), ((), ()))
_TN = (((0,), (0,)), ((), ()))


def _adaln_kernel(c_ref, w_ref, b_ref, o_ref):
    c = c_ref[...]
    o_ref[...] = _dot(c * _sigmoid(c), w_ref[...], precision=HI) + b_ref[...]


def _adaln(cond, w_mod, b_mod):
    n, d = cond.shape
    m = w_mod.shape[1]
    tn = 512
    return pl.pallas_call(
        _adaln_kernel,
        grid=(m // tn,),
        in_specs=[pl.BlockSpec((n, d), lambda j: (0, 0)),
                  pl.BlockSpec((d, tn), lambda j: (0, j)),
                  pl.BlockSpec((1, tn), lambda j: (0, j))],
        out_specs=pl.BlockSpec((n, tn), lambda j: (0, j)),
        out_shape=jax.ShapeDtypeStruct((n, m), F32),
        compiler_params=_cparams(("arbitrary",)),
        name="adaln",
    )(cond, w_mod, b_mod.reshape(1, m))


def _premix_kernel(x_ref, mod_ref, g_ref, w_ref, u_ref, qkv_ref, z_ref, ab_ref, *, n_u, n_qkv, n_z):
    m = mod_ref[0]
    h = _rms(x_ref[0], g_ref[...]) * (1.0 + m[1:2]) + m[0:1]
    p = _dot(h.astype(BF16), w_ref[...])
    for j in range(u_ref.shape[0]):
        u_ref[j] = p[:, j * LANES:(j + 1) * LANES].reshape(u_ref.shape[1:])
    qkv_ref[0] = p[:, n_u:n_u + n_qkv]
    z_ref[0] = p[:, n_u + n_qkv:n_u + n_qkv + n_z].astype(BF16)
    ab_ref[0] = p[:, n_u + n_qkv + n_z:]


def _premix(x, mod, per_batch, g, w_pad, n_u, n_qkv, n_z):
    b, l, d = x.shape
    tm = min(512, l)
    tc = S5_CHUNK
    n_all = w_pad.shape[1]
    n_ab = n_all - n_u - n_qkv - n_z
    mod_map = (lambda i, t: (i, 0, 0)) if per_batch else (lambda i, t: (0, 0, 0))
    tok = lambda i, t: (i, t, 0)
    return pl.pallas_call(
        functools.partial(_premix_kernel, n_u=n_u, n_qkv=n_qkv, n_z=n_z),
        grid=(b, l // tm),
        in_specs=[pl.BlockSpec((1, tm, d), tok),
                  pl.BlockSpec((1,) + mod.shape[1:], mod_map),
                  pl.BlockSpec((1, d), lambda i, t: (0, 0)),
                  pl.BlockSpec((d, n_all), lambda i, t: (0, 0))],
        out_specs=[pl.BlockSpec((n_u // LANES, tm // tc, None, tc, LANES), lambda i, t: (0, t, i, 0, 0)),
                   pl.BlockSpec((1, tm, n_qkv), tok),
                   pl.BlockSpec((1, tm, n_z), tok), pl.BlockSpec((1, tm, n_ab), tok)],
        out_shape=[jax.ShapeDtypeStruct((n_u // LANES, l // tc, b, tc, LANES), F32), jax.ShapeDtypeStruct((b, l, n_qkv), F32),
                   jax.ShapeDtypeStruct((b, l, n_z), BF16), jax.ShapeDtypeStruct((b, l, n_ab), F32)],
        compiler_params=_cparams(("parallel", "parallel")),
        name="premix",
    )(x, mod, g, w_pad)


def _s5_matrices(lam_re, lam_im, log_dt, b_re, b_im, c_re, c_im, d_skip):
    t = S5_CHUNK
    n_dirs, g, p = lam_re.shape
    cg = b_re.shape[-1]
    dt = jnp.exp(log_dt)[..., None]
    lr, li = lam_re, lam_im
    den = lr * lr + li * li
    mag = jnp.exp(lr * dt)
    lb_re, lb_im = mag * jnp.cos(li * dt), mag * jnp.sin(li * dt)
    nr = lb_re - 1.0
    cr = (nr * lr + lb_im * li) / den
    ci = (lb_im * lr - nr * li) / den
    bb_re = cr[..., None] * b_re - ci[..., None] * b_im
    bb_im = cr[..., None] * b_im + ci[..., None] * b_re
    taus = jnp.arange(t + 1, dtype=F32)[:, None, None, None]
    pmag = jnp.exp(taus * (lr * dt))
    pw_re = pmag * jnp.cos(taus * (li * dt))
    pw_im = pmag * jnp.sin(taus * (li * dt))
    ein = functools.partial(jnp.einsum, precision=HI)
    lbr = pw_re[..., None] * bb_re - pw_im[..., None] * bb_im
    lbi = pw_re[..., None] * bb_im + pw_im[..., None] * bb_re
    kk = ein('dgcp,tdgpe->tdgce', c_re, lbr) - ein('dgcp,tdgpe->tdgce', c_im, lbi)
    er = c_re[None] * pw_re[:, :, :, None, :] - c_im[None] * pw_im[:, :, :, None, :]
    ei = c_re[None] * pw_im[:, :, :, None, :] + c_im[None] * pw_re[:, :, :, None, :]

    s_idx = jnp.arange(t)
    lag = s_idx[None, :] - s_idx[:, None]
    kf = kk[:, 0][jnp.clip(lag, 0, t)]
    kb = kk[:, 1][jnp.clip(-lag, 0, t)]
    m_full = (jnp.where((lag >= 0)[:, :, None, None, None], kf, 0.0)
              + jnp.where((lag <= 0)[:, :, None, None, None], kb, 0.0))
    eye_t = jnp.eye(t, dtype=F32)[:, :, None, None, None]
    eye_c = jnp.eye(cg, dtype=F32)[None, None, None]
    m_full = m_full + eye_t * eye_c * d_skip.reshape(g, cg)[None, None, :, :, None]
    p_parts = (lbr[:t, 0][::-1], lbi[:t, 0][::-1], lbr[:t, 1], lbi[:t, 1])
    r_parts = (er[1:, 0], -ei[1:, 0], er[1:, 1][::-1], -ei[1:, 1][::-1])

    gw = LANES // cg
    n8 = g // gw
    def spread(width):
        lane = jnp.arange(gw * width)
        tile = (lane[None, :] % width == jnp.arange(width)[:, None]).astype(F32)
        own = (lane[None, :] // width == jnp.arange(gw)[:, None]).astype(F32)
        return tile, own[None, None, :, None, None, :]

    tile_c, own_c = spread(cg)
    tile_p, own_p = spread(p)
    m6 = m_full.reshape(t, t, n8, gw, cg, cg).transpose(2, 0, 3, 5, 1, 4)
    mm = (ein('xsgctd,dl->xsgctl', m6, tile_c) * own_c).reshape(n8, t * LANES, t * LANES)
    p7 = jnp.stack([a.reshape(t, n8, gw, p, cg).transpose(1, 0, 2, 4, 3) for a in p_parts], axis=4)
    pq = (ein('xsgcqn,nl->xsgcql', p7, tile_p) * own_p).reshape(n8, t * LANES, 4 * gw * p)
    r7 = jnp.stack([a.reshape(t, n8, gw, cg, p).transpose(1, 2, 4, 0, 3) for a in r_parts], axis=1)
    rr = (ein('xqgntd,dl->xqgntl', r7, tile_c) * own_c).reshape(n8, 4 * gw * p, t * LANES)
    dec = jnp.stack([x.reshape(n8, gw * p) for x in (pw_re[t, 0], pw_im[t, 0], pw_re[t, 1], pw_im[t, 1])], axis=1)
    dec = jnp.concatenate([dec, jnp.zeros_like(dec)], axis=1)
    return pq.astype(BF16), mm.astype(BF16), rr.astype(BF16), dec


def _chunk_rows(ref, nb, t):
    n = ref.shape[0]
    return jnp.concatenate([ref[:, :, s, :].reshape(n * nb, LANES) for s in range(t)], axis=1).astype(BF16)


def _s5_state_kernel(x_ref, c_ref, pq_ref, dec_ref, sin_ref, loc_ref, locc_ref, *, nb, t, nct, ncx, ncc, sw):
    rt = pl.program_id(1)
    rows = nct * nb
    loc_ref[pl.ds(pl.multiple_of(rt * rows, rows), rows), :] = _dot(_chunk_rows(x_ref, nb, t), pq_ref[0])

    @pl.when(rt == 0)
    def _():
        locc_ref[...] = _dot(_chunk_rows(c_ref, nb, t), pq_ref[0])

    @pl.when(rt == pl.num_programs(1) - 1)
    def _():
        dec = dec_ref[0]
        far, fai, bar, bai = (jnp.broadcast_to(dec[i:i + 1], (nb, sw)) for i in range(4))

        def step(state, ar, ai, ref, k, c0):
            re, im = state
            r = pl.ds(pl.multiple_of(k * nb, nb), nb)
            return ar * re - ai * im + ref[r, c0:c0 + sw], ar * im + ai * re + ref[r, c0 + sw:c0 + 2 * sw]

        zero = jnp.zeros((nb, sw), F32)

        def ctx_body(i, carry):
            f, bk = carry
            return step(f, far, fai, locc_ref, i, 0), step(bk, bar, bai, locc_ref, ncc - 1 - i, 2 * sw)

        carry = lax.fori_loop(0, ncc, ctx_body, ((zero, zero), (zero, zero)))

        def pair_store(k0, c0, s_lo, s_hi):
            r = pl.ds(pl.multiple_of(k0 * nb, 2 * nb), 2 * nb)
            sin_ref[0, r, c0:c0 + sw] = jnp.concatenate([s_lo[0], s_hi[0]], axis=0).astype(BF16)
            sin_ref[0, r, c0 + sw:c0 + 2 * sw] = jnp.concatenate([s_lo[1], s_hi[1]], axis=0).astype(BF16)

        def x_body(j, carry):
            f0, b0 = carry
            f1 = step(f0, far, fai, loc_ref, 2 * j, 0)
            pair_store(2 * j, 0, f0, f1)
            kb = ncx - 2 - 2 * j
            b1 = step(b0, bar, bai, loc_ref, kb + 1, 2 * sw)
            pair_store(kb, 2 * sw, b1, b0)
            return step(f1, far, fai, loc_ref, 2 * j + 1, 0), step(b1, bar, bai, loc_ref, kb, 2 * sw)

        lax.fori_loop(0, ncx // 2, x_body, carry)


def _s5_out_kernel(x_ref, sin_ref, mm_ref, rr_ref, y_ref, *, nb, t):
    n = x_ref.shape[0]
    y = _dot(_chunk_rows(x_ref, nb, t), mm_ref[0]) + _dot(sin_ref[0], rr_ref[0])
    for s in range(t):
        y_ref[:, :, s, :] = y[:, s * LANES:(s + 1) * LANES].reshape(n, nb, LANES)


def _s5(ux, uc, mats):
    pq, mm, rr, dec = mats
    n8, ncx, nb, t, _ = ux.shape
    ncc = uc.shape[1]
    sw = dec.shape[-1]
    assert nb % 8 == 0 and t == S5_CHUNK and ncx % 2 == 0 and n8 == pq.shape[0]
    nct = min(64, ncx)
    ux5, uc5 = ux, uc
    kdim = t * LANES
    xblk = pl.BlockSpec((None, nct, nb, t, LANES), lambda g, r: (g, r, 0, 0, 0))
    wmap = lambda g, r: (g, 0, 0)
    sin = pl.pallas_call(
        functools.partial(_s5_state_kernel, nb=nb, t=t, nct=nct, ncx=ncx, ncc=ncc, sw=sw),
        grid=(n8, ncx // nct),
        in_specs=[xblk, pl.BlockSpec((None, ncc, nb, t, LANES), lambda g, r: (g, 0, 0, 0, 0)),
                  pl.BlockSpec((1, kdim, 4 * sw), wmap), pl.BlockSpec((1, 8, sw), wmap)],
        out_specs=pl.BlockSpec((1, ncx * nb, 4 * sw), wmap),
        out_shape=jax.ShapeDtypeStruct((n8, ncx * nb, 4 * sw), BF16),
        scratch_shapes=[pltpu.VMEM((ncx * nb, 4 * sw), F32), pltpu.VMEM((ncc * nb, 4 * sw), F32)],
        compiler_params=_cparams(("parallel", "arbitrary")),
        name="s5_state",
    )(ux5, uc5, pq, dec)
    y = pl.pallas_call(
        functools.partial(_s5_out_kernel, nb=nb, t=t),
        grid=(n8, ncx // nct),
        in_specs=[xblk, pl.BlockSpec((1, nct * nb, 4 * sw), lambda g, r: (g, r, 0)),
                  pl.BlockSpec((1, kdim, kdim), wmap), pl.BlockSpec((1, 4 * sw, kdim), wmap)],
        out_specs=xblk,
        out_shape=jax.ShapeDtypeStruct((n8, ncx, nb, t, LANES), F32),
        compiler_params=_cparams(("parallel", "parallel")),
        name="s5_out",
    )(ux5, sin, mm, rr)
    return y


_CONV_PAD = 72


def _gdn_conv_kernel(x_ref, w_ref, o_ref, pad_ref, *, length, cols, two_d, n_norm, n_q, q_scale):
    cb = pl.program_id(1)
    pad = _CONV_PAD
    zeros = jnp.zeros((pad, LANES), F32)
    pad_ref[0:pad, :] = zeros
    pad_ref[pad + length:pad + length + pad, :] = zeros
    pad_ref[pad:pad + length, :] = x_ref[0]
    rc = min(256, length)
    for r0 in range(0, length, rc):
        col = lax.rem(lax.broadcasted_iota(jnp.int32, (rc, LANES), 0) + r0, cols)
        acc = jnp.zeros((rc, LANES), F32)
        for dr in ((0, 1, 2) if two_d else (1,)):
            for dc in range(3):
                off = (dr - 1) * cols + (dc - 1)
                term = pad_ref[pad + r0 + off:pad + r0 + off + rc, :] * w_ref[dr * 3 + dc:dr * 3 + dc + 1, :]
                if dc == 0:
                    term = jnp.where(col >= 1, term, 0.0)
                if dc == 2:
                    term = jnp.where(col <= cols - 2, term, 0.0)
                acc = acc + term
        y = acc * _sigmoid(acc)
        nrm = y * lax.rsqrt(jnp.sum(y * y, axis=-1, keepdims=True) + EPS)
        nrm = nrm * jnp.where(cb < n_q, q_scale, 1.0)
        o_ref[0, r0:r0 + rc, :] = jnp.where(cb < n_norm, nrm, y)


def _gdn_conv(qkv, conv_w, cols, two_d, n_heads, head_dim):
    b, l, ch = qkv.shape
    w9 = jnp.concatenate([conv_w.reshape(9, ch), jnp.zeros((7, ch), F32)], axis=0)
    blk = lambda i, j: (i, 0, j)
    return pl.pallas_call(
        functools.partial(_gdn_conv_kernel, length=l, cols=cols, two_d=two_d,
                          n_norm=2 * n_heads, n_q=n_heads, q_scale=head_dim ** -0.5),
        grid=(b, ch // LANES),
        in_specs=[pl.BlockSpec((1, l, LANES), blk), pl.BlockSpec((16, LANES), lambda i, j: (0, j))],
        out_specs=pl.BlockSpec((1, l, LANES), blk),
        out_shape=jax.ShapeDtypeStruct((b, l, ch), F32),
        scratch_shapes=[pltpu.VMEM((l + 2 * _CONV_PAD, LANES), F32)],
        compiler_params=_cparams(("parallel", "parallel")),
        name="gdn_conv",
    )(qkv, w9)


def _gdn_kernel(qx_ref, kx_ref, vx_ref, abx_ref, qc_ref, kc_ref, vc_ref, abc_ref,
                pl_ref, o_ref, gcx_ref, bcx_ref, gcc_ref, bcc_ref,
                u_ref, w_ref, qd_ref, kd_ref, at_ref, gt_ref, *, n_heads, ncx, ncc):
    cs = GDN_CHUNK
    hd = qx_ref.shape[-1]
    head = pl.program_id(1)
    alog_l, dtb_l = pl_ref[0:1, :], pl_ref[1:2, :]
    lane = lax.broadcasted_iota(jnp.int32, (1, LANES), 1)

    def gate_columns(ab_ref, g_out, b_out, n_rows):
        rc = min(256, n_rows)
        for r0 in range(0, n_rows, rc):
            ab = ab_ref[0, r0:r0 + rc, :]
            g_all = -jnp.exp(alog_l) * _softplus(ab + dtb_l)
            b_all = _sigmoid(ab)
            for dr in range(2):
                j = dr * n_heads + head
                g = jnp.sum(jnp.where(lane == j, g_all, 0.0), axis=-1, keepdims=True)
                bt = jnp.sum(jnp.where(lane == j + 2 * n_heads, b_all, 0.0), axis=-1, keepdims=True)
                g_out[dr, r0:r0 + rc, :] = jnp.broadcast_to(g, (rc, hd))
                b_out[dr, r0:r0 + rc, :] = jnp.broadcast_to(bt, (rc, hd))

    gate_columns(abx_ref, gcx_ref, bcx_ref, ncx * cs)
    gate_columns(abc_ref, gcc_ref, bcc_ref, ncc * cs)

    nblk = 4
    side = nblk * cs
    def block_masks():
        ri = lax.broadcasted_iota(jnp.int32, (side, side), 0)
        ci = lax.broadcasted_iota(jnp.int32, (side, side), 1)
        dist = jnp.where(ri < side // 2, ri - ci, ci - ri)
        dist = jnp.where((ri // cs) == (ci // cs), dist, -1)
        return dist >= 0, dist > 0, jnp.where(ri == ci, 1.0, 0.0)

    def cumsum_blocks(x):
        row_id = lax.broadcasted_iota(jnp.int32, (side, hd), 0)
        pos = row_id % cs
        is_fwd = row_id < side // 2
        s = 1
        while s < cs:
            x = x + jnp.where(is_fwd, jnp.where(pos >= s, pltpu.roll(x, s, 0), 0.0),
                              jnp.where(pos < cs - s, pltpu.roll(x, side - s, 0), 0.0))
            s *= 2
        return x

    def each(f, *lists):
        return [f(*args) for args in zip(*lists)]

    def chunks_local(c0s, refs, base):
        q_ref, k_ref, v_ref, gcol_ref, bcol_ref = refs
        incl, strict, eye = block_masks()
        rows = [pl.ds(pl.multiple_of(c0 * cs, 2 * cs), 2 * cs) for c0 in c0s]
        q, k, v = ([jnp.concatenate([r[0, rw, :]] * 2, axis=0) for rw in rows] for r in (q_ref, k_ref, v_ref))
        gb, bb = ([jnp.concatenate([r[0, rw, :], r[1, rw, :]], axis=0) for rw in rows] for r in (gcol_ref, bcol_ref))
        gcs = each(cumsum_blocks, gb)
        gwide = each(lambda g: jnp.concatenate([g] * (side // hd), axis=1), gcs)
        decay = each(lambda g: jnp.where(incl, jnp.exp(jnp.where(incl, g - g.T[0:1, :], 0.0)), 0.0), gwide)
        kbf = each(lambda t: t.astype(BF16), k)
        kb = each(lambda t, b_: t * b_, k, bb)
        lmat = each(lambda x, y, dc: jnp.where(strict, _dot(x.astype(BF16), y, _NT) * dc, 0.0), kb, kbf, decay)
        pw = each(lambda l_: -l_, lmat)
        tmat = each(lambda p_: eye + p_, pw)
        for _ in range(cs.bit_length() - 2):
            pw = each(lambda p_: _dot(p_.astype(BF16), p_.astype(BF16)), pw)
            tmat = each(lambda t_, p_: t_ + _dot(t_.astype(BF16), p_.astype(BF16)), tmat, pw)
        g_exp = each(jnp.exp, gcs)
        rhs = each(lambda v_, b_, kb_, ge: jnp.concatenate([v_ * b_, kb_ * ge], axis=1).astype(BF16), v, bb, kb, g_exp)
        uw = each(lambda t_, r_: _dot(t_.astype(BF16), r_), tmat, rhs)
        attn = each(lambda q_, y, dc: jnp.where(incl, _dot(q_.astype(BF16), y, _NT) * dc, 0.0), q, kbf, decay)
        half = side // 2
        for c0, uw_, attn_, q_, k_, g_, ge in zip(c0s, uw, attn, q, k, gcs, g_exp):
            lasts = [g_[(j + 1) * cs - 1:(j + 1) * cs, :] if j < nblk // 2 else g_[j * cs:j * cs + 1, :] for j in range(nblk)]
            g_last = jnp.concatenate([jnp.broadcast_to(r, (cs, hd)) for r in lasts], axis=0)
            qd = (q_ * ge).astype(BF16)
            kd = (k_ * jnp.exp(g_last - g_)).astype(BF16)
            srows = pl.ds(pl.multiple_of((base + c0) * cs, 2 * cs), 2 * cs)
            grows = pl.ds(pl.multiple_of((base + c0) * 8, 16), 16)
            for dr in range(2):
                blk = slice(dr * half, (dr + 1) * half)
                u_ref[dr, srows, :] = uw_[blk, :hd]
                w_ref[dr, srows, :] = uw_[blk, hd:].astype(BF16)
                qd_ref[dr, srows, :] = qd[blk]
                kd_ref[dr, srows, :] = kd[blk]
                at_ref[dr, srows, :] = jnp.concatenate(
                    [attn_[j * cs:(j + 1) * cs, j * cs:(j + 1) * cs] for j in (2 * dr, 2 * dr + 1)], axis=0).astype(BF16)
                gt_ref[dr, grows, :] = jnp.concatenate(
                    [jnp.broadcast_to(jnp.exp(lasts[j]), (8, hd)) for j in (2 * dr, 2 * dr + 1)], axis=0)

    def chunks_seq(cgs, states):
        drs = (0, 1)
        srows = [pl.ds(pl.multiple_of(cg * cs, cs), cs) for cg in cgs]
        sb = [st.astype(BF16) for st in states]
        vb = [(u_ref[dr, r, :] - _dot(w_ref[dr, r, :], s_)).astype(BF16) for dr, r, s_ in zip(drs, srows, sb)]
        new = [st * gt_ref[dr, pl.ds(pl.multiple_of(cg * 8, 8), 8), :][0:1] + _dot(kd_ref[dr, r, :], v_, _TN)
               for dr, cg, r, st, v_ in zip(drs, cgs, srows, states, vb)]
        outs = [_dot(qd_ref[dr, r, :], s_) + _dot(at_ref[dr, r, :], v_) for dr, r, s_, v_ in zip(drs, srows, sb, vb)]
        return outs, new

    refs_c = (qc_ref, kc_ref, vc_ref, gcc_ref, bcc_ref)
    refs_x = (qx_ref, kx_ref, vx_ref, gcx_ref, bcx_ref)

    def local_pass(n, refs, base):
        per = 8 if n % 8 == 0 else 4

        def body(i, _):
            chunks_local([per * i + 2 * j for j in range(per // 2)], refs, base)
            return 0

        lax.fori_loop(0, n // per, body, 0)

    local_pass(ncc, refs_c, 0)
    local_pass(ncx, refs_x, ncc)

    zero = jnp.zeros((hd, hd), F32)

    def ctx_body(i, carry):
        return tuple(chunks_seq((i, ncc - 1 - i), carry)[1])

    carry = lax.fori_loop(0, ncc, ctx_body, (zero, zero))
    o_ref[...] = jnp.zeros_like(o_ref)

    def x_body(i, carry):
        ib = ncx - 1 - i
        (of, ob), new = chunks_seq((ncc + i, ncc + ib), carry)
        o_ref[0, pl.ds(pl.multiple_of(i * cs, cs), cs), :] += of
        o_ref[0, pl.ds(pl.multiple_of(ib * cs, cs), cs), :] += ob
        return tuple(new)

    lax.fori_loop(0, ncx, x_body, carry)


def _gdn(qkvx, abx, qkvc, abc, a_log, dt_bias, n_heads, head_dim):
    b, l, _ = qkvx.shape
    lc = qkvc.shape[1]
    cs = GDN_CHUNK
    ncx, ncc = l // cs, lc // cs
    assert ncx % 4 == 0 and ncc % 4 == 0, "chunks are processed four at a time"
    ng = 2 * n_heads
    p_l = jnp.zeros((8, LANES), F32)
    p_l = p_l.at[0, :ng].set(a_log.reshape(ng)).at[1, :ng].set(dt_bias.reshape(ng))

    def col(off):
        return lambda i, h: (i, 0, off + h)

    full = lambda i, h: (i, 0, 0)
    const = lambda i, h: (0, 0)
    hb = head_dim
    return pl.pallas_call(
        functools.partial(_gdn_kernel, n_heads=n_heads, ncx=ncx, ncc=ncc),
        grid=(b, n_heads),
        in_specs=[pl.BlockSpec((1, l, hb), col(0)), pl.BlockSpec((1, l, hb), col(n_heads)),
                  pl.BlockSpec((1, l, hb), col(2 * n_heads)),
                  pl.BlockSpec((1, l, LANES), full),
                  pl.BlockSpec((1, lc, hb), col(0)), pl.BlockSpec((1, lc, hb), col(n_heads)),
                  pl.BlockSpec((1, lc, hb), col(2 * n_heads)),
                  pl.BlockSpec((1, lc, LANES), full),
                  pl.BlockSpec((8, LANES), const)],
        out_specs=pl.BlockSpec((1, l, hb), col(0)),
        out_shape=jax.ShapeDtypeStruct((b, l, n_heads * head_dim), F32),
        scratch_shapes=[pltpu.VMEM((2, l, hb), F32), pltpu.VMEM((2, l, hb), F32),
                        pltpu.VMEM((2, lc, hb), F32), pltpu.VMEM((2, lc, hb), F32),
                        pltpu.VMEM((2, l + lc, hb), F32), pltpu.VMEM((2, l + lc, hb), BF16),
                        pltpu.VMEM((2, l + lc, hb), BF16), pltpu.VMEM((2, l + lc, hb), BF16),
                        pltpu.VMEM((2, l + lc, cs), BF16), pltpu.VMEM((2, (ncx + ncc) * 8, hb), F32)],
        compiler_params=_cparams(("parallel", "arbitrary")),
        name="gdn",
    )(qkvx, qkvx, qkvx, abx, qkvc, qkvc, qkvc, abc, p_l)


def _postmix_kernel(x_ref, y_ref, o_ref, z_ref, mod_ref, gn_ref, gpost_ref, wglu_ref, wout_ref, out_ref,
                    *, n_heads, head_dim, s5_w):
    s = _gelu(jnp.concatenate([y_ref[j].reshape(x_ref.shape[1], LANES) for j in range(y_ref.shape[0])], axis=1))
    s = s * _sigmoid(_dot(s.astype(BF16), wglu_ref[...]))
    o = o_ref[0]
    z = z_ref[0].astype(F32)
    parts = []
    for h in range(n_heads):
        sl = slice(h * head_dim, (h + 1) * head_dim)
        zh = z[:, sl]
        parts.append(_rms(o[:, sl], gn_ref[...]) * (zh * _sigmoid(zh)))
    gd = jnp.concatenate(parts, axis=-1)
    mixed = _dot(s.astype(BF16), wout_ref[0:s5_w, :]) + _dot(gd.astype(BF16), wout_ref[s5_w:, :])
    gate = mod_ref[0][2:3]
    out_ref[0] = x_ref[0] + gate * _rms(mixed, gpost_ref[...])


def _postmix(x, y, o, z, mod, gdn_norm, g_post, w_glu, w_out, n_heads, head_dim):
    b, l, d = x.shape
    s5_w = y.shape[0] * y.shape[-1]
    gw = o.shape[-1]
    tm = min(512, l)
    tc = S5_CHUNK
    tok = lambda i, t: (i, t, 0)
    const = lambda i, t: (0, 0)
    return pl.pallas_call(
        functools.partial(_postmix_kernel, n_heads=n_heads, head_dim=head_dim, s5_w=s5_w),
        grid=(b, l // tm),
        in_specs=[pl.BlockSpec((1, tm, d), tok), pl.BlockSpec((s5_w // LANES, tm // tc, None, tc, LANES), lambda i, t: (0, t, i, 0, 0)),
                  pl.BlockSpec((1, tm, gw), tok), pl.BlockSpec((1, tm, gw), tok),
                  pl.BlockSpec((1,) + mod.shape[1:], lambda i, t: (i, 0, 0)),
                  pl.BlockSpec((1, head_dim), const), pl.BlockSpec((1, d), const),
                  pl.BlockSpec((s5_w, s5_w), const), pl.BlockSpec((s5_w + gw, d), const)],
        out_specs=pl.BlockSpec((1, tm, d), tok),
        out_shape=jax.ShapeDtypeStruct((b, l, d), F32),
        compiler_params=_cparams(("parallel", "parallel")),
        name="postmix",
    )(x, y, o, z, mod, gdn_norm, g_post, w_glu, w_out)


def _peer_prep_kernel(x_ref, mod_ref, g_ref, wq_ref, keys_ref, ht_ref, r2_ref, e2_ref, n1_ref, e1_ref,
                      s1_ref, s2_ref, lv1_ref, lv2_ref, tau_ref, *, n_keys, topk, tt):
    @pl.when(pl.program_id(1) == 0)
    def _():
        m = mod_ref[0]
        h = _rms(x_ref[0], g_ref[...]) * (1.0 + m[4:5]) + m[3:4]
        ht_ref[...] = h.T.astype(BF16)

    half = keys_ref.shape[-1]
    ht = ht_ref[...]
    q1 = _dot(wq_ref[0:half, :], ht).astype(BF16)
    q2 = _dot(wq_ref[half:2 * half, :], ht).astype(BF16)
    s1_ref[...] = _dot(keys_ref[0, 0], q1)
    s2_ref[...] = _dot(keys_ref[0, 1], q2)
    kf = float(topk)

    nt = tt // LANES

    def lane_tile(t):
        return pl.ds(pl.multiple_of(t * LANES, LANES), LANES)

    def levels(t, _):
        cols = lane_tile(t)

        def lv_body(q, carry):
            c1, c2, rank2 = carry
            m1 = jnp.max(c1, axis=0, keepdims=True)
            m2 = jnp.max(c2, axis=0, keepdims=True)
            lv1_ref[t, pl.ds(q, 1), :] = m1
            lv2_ref[t, pl.ds(q, 1), :] = m2
            h1 = c1 == jnp.where(m1 > NEG, m1, -NEG)
            h2 = c2 == jnp.where(m2 > NEG, m2, -NEG)
            rank2 = jnp.where(h2, jnp.asarray(q, F32), rank2)
            return jnp.where(h1, NEG, c1), jnp.where(h2, NEG, c2), rank2

        s1, s2 = s1_ref[:, cols], s2_ref[:, cols]
        _, _, rank2 = lax.fori_loop(0, topk, lv_body, (s1, s2, jnp.full(s2.shape, kf, F32)))
        r2_ref[:, cols] = rank2.astype(BF16)
        return 0

    lax.fori_loop(0, nt, levels, 0)

    a_all, b_all = (jnp.concatenate([r[j] for j in range(nt)], axis=1) for r in (lv1_ref, lv2_ref))
    sub = topk // 2
    cand = jnp.concatenate([a_all[0:1] + b_all] + [a_all[p:p + 1] + b_all[0:sub] for p in range(1, sub)]
                           + [a_all[sub:] + b_all[0:1]], axis=0)

    def tau_body(_, carry):
        cand, cnt, tau = carry
        mx = jnp.max(cand, axis=0, keepdims=True)
        hit = cand == mx
        tau = jnp.where(cnt < kf, mx, tau)
        cnt = cnt + jnp.sum(jnp.where(hit, 1.0, 0.0), axis=0, keepdims=True)
        return jnp.where(hit, NEG, cand), cnt, tau

    row0 = a_all[0:1]
    _, _, tau_all = lax.fori_loop(0, topk, tau_body, (cand, jnp.zeros_like(row0), jnp.full_like(row0, NEG)))
    for j in range(nt):
        tau_ref[j] = jnp.broadcast_to(tau_all[:, j * LANES:(j + 1) * LANES], tau_ref.shape[1:])

    def gates(t, _):
        cols = lane_tile(t)
        s1, s2 = s1_ref[:, cols], s2_ref[:, cols]
        a_mat, b_mat, tau = lv1_ref[t], lv2_ref[t], tau_ref[t][0:1]
        a0, b0 = a_mat[0:1], b_mat[0:1]

        def n_body(q, carry):
            n1, c2 = carry
            b = lv2_ref[t, pl.ds(q, 1), :]
            n1 = n1 + jnp.where(s1 + b >= tau, 1.0, 0.0)
            c2 = c2 + jnp.where(a_mat + b >= tau, jnp.exp(b - b0), 0.0)
            return n1, c2

        n1, c2 = lax.fori_loop(0, topk, n_body, (jnp.zeros_like(s1), jnp.zeros_like(a_mat)))
        z = jnp.sum(jnp.exp(a_mat - a0) * c2, axis=0, keepdims=True)
        e2_ref[:, cols] = (jnp.exp(s2 - b0) / z).astype(BF16)
        n1_ref[0, :, cols] = n1
        e1_ref[0, :, cols] = jnp.exp(s1 - a0)
        return 0

    lax.fori_loop(0, nt, gates, 0)


def _peer_main_kernel(ht_ref, u_ref, vt_ref, r2_ref, e2_ref, n1_ref, e1_ref, x_ref, mod_ref, g_ref,
                      out_ref, acc_ref, hs0_ref, hs1_ref, act0_ref, act1_ref, r2s_ref, e2s_ref, fence_ref,
                      *, n_heads, n_keys, eb, tt):
    k = pl.program_id(1)
    nsub = eb // n_keys
    pk = BF16_ROWS
    grp = n_keys // pk

    last = pl.num_programs(1) - 1

    @pl.when(k == 0)
    def _():
        acc_ref[...] = jnp.zeros_like(acc_ref)
        r2s_ref[...] = r2_ref[...]
        e2s_ref[...] = e2_ref[...]

    def gate_tiles(hs_ref, act_ref, blk, ii):
        i = blk * nsub + ii
        nrows, erows = n1_ref[i], e1_ref[i]
        rs = slice(ii * n_keys, (ii + 1) * n_keys)
        for tc in range(tt // LANES):
            cs_ = slice(tc * LANES, (tc + 1) * LANES)
            gsum = jnp.zeros((n_keys, LANES), BF16)
            for hd in range(n_heads):
                nb, ebc = (jnp.concatenate([jnp.broadcast_to(r[hd:hd + 1, cs_], (pk, LANES)).astype(BF16)] * grp, 0)
                           for r in (nrows, erows))
                ks = slice(hd * n_keys, (hd + 1) * n_keys)
                r2, e2 = r2s_ref[ks, cs_], e2s_ref[ks, cs_]
                gsum = gsum + jnp.where(r2 < nb, e2, jnp.zeros_like(e2)) * ebc
            act_ref[rs, cs_] = _gelu(hs_ref[rs, cs_]).astype(BF16) * gsum

    d = ht_ref.shape[0]
    npre, nmix = 4, 2
    kc = d // npre
    ec = eb // nmix

    def half_step(u_rows, hs_w, hs_r, act_w, act_r, blk, pre=True, gate=True, mix=True):
        for ii in range(nsub):
            if pre and ii % (nsub // npre) == 0:
                p = ii // (nsub // npre)
                u_part = pltpu.bitcast(u_ref[u_rows // 2:(u_rows + eb) // 2, p * kc:(p + 1) * kc], BF16)
                part = _dot(u_part, ht_ref[p * kc:(p + 1) * kc, :])
                if p == 0:
                    hs_w[...] = part
                else:
                    hs_w[...] += part
            if gate:
                gate_tiles(hs_r, act_w, blk, ii)
            if mix and ii % (nsub // nmix) == 0:
                j = ii // (nsub // nmix)
                vt_part = pltpu.bitcast(vt_ref[:, u_rows + j * ec:u_rows + (j + 1) * ec], BF16)
                acc_ref[...] += _dot(vt_part, act_r[j * ec:(j + 1) * ec, :])

    def step(first, final):
        half_step(0, hs0_ref, hs1_ref, act1_ref, act0_ref, 2 * k - 1, pre=not final, gate=not first, mix=not first)
        pl.semaphore_signal(fence_ref, 1)
        pl.semaphore_wait(fence_ref, 1)
        half_step(eb, hs1_ref, hs0_ref, act0_ref, act1_ref, 2 * k, pre=not final, gate=not final, mix=not first)

    @pl.when(k == 0)
    def _():
        step(True, False)

    @pl.when((k > 0) & (k < last))
    def _():
        step(False, False)

    @pl.when(k == last)
    def _():
        step(False, True)
        o = acc_ref[...].T
        out_ref[0] = x_ref[0] + mod_ref[0][5:6] * _rms(o, g_ref[...])


def _peer(x1, mod, g_pre, g_post, w_q, keys, exp_u, exp_v):
    b, l, d = x1.shape
    n_heads, _, n_keys, half = keys.shape
    n_exp = exp_u.shape[0]
    ntok = b * l
    tt = min(512, l)
    eb = 1024
    tpb = l // tt
    wq_t = w_q.T.astype(BF16)
    keys_b = keys.astype(BF16)
    u_b = _pack_rows(exp_u, False)
    vt_b = _pack_rows(exp_v, True)
    tokb = lambda t, h: (t // tpb, t % tpb, 0)
    modb = lambda t, h: (t // tpb, 0, 0)
    hmap = lambda t, h: (h, 0, t)
    hshape = jax.ShapeDtypeStruct((n_heads, n_keys, ntok), F32)
    hshape_b = jax.ShapeDtypeStruct((n_heads * n_keys, ntok), BF16)
    ht, r2, e2, n1, e1 = pl.pallas_call(
        functools.partial(_peer_prep_kernel, n_keys=n_keys, topk=PEER_TOPK, tt=tt),
        grid=(ntok // tt, n_heads),
        in_specs=[pl.BlockSpec((1, tt, d), tokb), pl.BlockSpec((1,) + mod.shape[1:], modb),
                  pl.BlockSpec((1, d), lambda t, h: (0, 0)),
                  pl.BlockSpec((2 * half, d), lambda t, h: (h, 0)),
                  pl.BlockSpec((1, 2, n_keys, half), lambda t, h: (h, 0, 0, 0))],
        out_specs=[pl.BlockSpec((d, tt), lambda t, h: (0, t))] + [pl.BlockSpec((n_keys, tt), lambda t, h: (h, t))] * 2
                  + [pl.BlockSpec((1, n_keys, tt), hmap)] * 2,
        out_shape=[jax.ShapeDtypeStruct((d, ntok), BF16), hshape_b, hshape_b, hshape, hshape],
        scratch_shapes=[pltpu.VMEM((n_keys, tt), F32), pltpu.VMEM((n_keys, tt), F32),
                        pltpu.VMEM((tt // LANES, PEER_TOPK, LANES), F32), pltpu.VMEM((tt // LANES, PEER_TOPK, LANES), F32),
                        pltpu.VMEM((tt // LANES, 8, LANES), F32)],
        compiler_params=_cparams(("parallel", "arbitrary")),
        name="peer_prep",
    )(x1, mod, g_pre, wq_t, keys_b)

    n_pairs = n_exp // (2 * eb)
    tokb2 = lambda t, k: (t // tpb, t % tpb, 0)
    hmap2 = lambda t, k: (0, 0, t)
    return pl.pallas_call(
        functools.partial(_peer_main_kernel, n_heads=n_heads, n_keys=n_keys, eb=eb, tt=tt),
        grid=(ntok // tt, n_pairs + 1),
        in_specs=[pl.BlockSpec((d, tt), lambda t, k: (0, t)),
                  pl.BlockSpec((eb, d), lambda t, k: (jnp.minimum(k, n_pairs - 1), 0)),
                  pl.BlockSpec((d // 2, 2 * eb), lambda t, k: (0, jnp.maximum(k - 1, 0)))]
                 + [pl.BlockSpec((n_heads * n_keys, tt), lambda t, k: (0, t))] * 2
                 + [pl.BlockSpec((n_keys, n_heads, tt), hmap2)] * 2
                 + [pl.BlockSpec((1, tt, d), tokb2),
                    pl.BlockSpec((1,) + mod.shape[1:], lambda t, k: (t // tpb, 0, 0)),
                    pl.BlockSpec((1, d), lambda t, k: (0, 0))],
        out_specs=pl.BlockSpec((1, tt, d), tokb2),
        out_shape=jax.ShapeDtypeStruct((b, l, d), F32),
        scratch_shapes=[pltpu.VMEM((d, tt), F32), pltpu.VMEM((eb, tt), F32), pltpu.VMEM((eb, tt), F32),
                        pltpu.VMEM((eb, tt), BF16), pltpu.VMEM((eb, tt), BF16),
                        pltpu.VMEM((n_heads * n_keys, tt), BF16), pltpu.VMEM((n_heads * n_keys, tt), BF16),
                        pltpu.SemaphoreType.REGULAR],
        compiler_params=_cparams(("parallel", "arbitrary")),
        name="peer_main",
    )(ht, u_b, vt_b, r2, e2, n1.transpose(1, 0, 2), e1.transpose(1, 0, 2), x1, mod, g_post)


def kernel(x, c, ctx, c_ctx, w_mod, b_mod, g_pre_mix, g_post_mix, g_pre_ffn, g_post_ffn, w_in, w_out,
           s5_lam_re, s5_lam_im, s5_log_dt, s5_b_re, s5_b_im, s5_c_re, s5_c_im, s5_d, s5_w_glu,
           gdn_conv, gdn_a_log, gdn_dt_bias, gdn_norm, peer_w_q, peer_keys, peer_u, peer_v):
    assert w_mod.shape[0] == 1, "single layer: the context stream only feeds the latent scans"
    b, l, d = x.shape
    lc = ctx.shape[1]
    n_heads = gdn_a_log.shape[-1]
    head_dim = gdn_norm.shape[-1]
    gdn_w = n_heads * head_dim
    s5_w = s5_d.shape[-1]
    n_gate = 2 * 2 * n_heads

    n_cond = -(-(b + 1) // 8) * 8
    cond = jnp.concatenate([c, c_ctx[None], jnp.zeros((n_cond - b - 1, d), F32)], axis=0)
    mod = _adaln(cond, w_mod[0], b_mod[0]).reshape(n_cond, 6, d)

    n_proj = w_in.shape[-1]
    w_pad = jnp.pad(w_in[0], ((0, 0), (0, -n_proj % LANES))).astype(BF16)
    g1 = g_pre_mix[0][None]
    ux, qkvx, zx, abx = _premix(x, mod[:b], True, g1, w_pad, s5_w, 3 * gdn_w, gdn_w)
    uc, qkvc, _, abc = _premix(ctx, mod[b:b + 1], False, g1, w_pad, s5_w, 3 * gdn_w, gdn_w)
    assert abx.shape[-1] == LANES and n_gate <= LANES

    mats = _s5_matrices(s5_lam_re[0], s5_lam_im[0], s5_log_dt[0], s5_b_re[0], s5_b_im[0],
                        s5_c_re[0], s5_c_im[0], s5_d[0])
    y_s5 = _s5(ux, uc, mats)

    qkvx = _gdn_conv(qkvx, gdn_conv[0], GRID_W, l // GRID_W > 1, n_heads, head_dim)
    qkvc = _gdn_conv(qkvc, gdn_conv[0], lc, False, n_heads, head_dim)
    o_gdn = _gdn(qkvx, abx, qkvc, abc, gdn_a_log[0], gdn_dt_bias[0], n_heads, head_dim)

    x1 = _postmix(x, y_s5, o_gdn, zx, mod[:b], gdn_norm[0][None], g_post_mix[0][None],
                  s5_w_glu[0].astype(BF16), w_out[0].astype(BF16), n_heads, head_dim)
    return _peer(x1, mod[:b], g_pre_ffn[0][None], g_post_ffn[0][None],
                 peer_w_q[0], peer_keys[0], peer_u[0], peer_v[0])
```

```python
import functools

import jax
import jax.numpy as jnp
from jax import lax
from jax.experimental import pallas as pl
from jax.experimental.pallas import tpu as pltpu

F32 = jnp.float32
BF16 = jnp.bfloat16
HI = lax.Precision.HIGHEST

EPS = 1e-6
GRID_W = 64
GDN_CHUNK = 64
S5_CHUNK = 8
PEER_TOPK = 16
LANES = 128
BF16_ROWS = 16
NEG = -1e30
VMEM_LIMIT = 56 * 1024 * 1024


def _cparams(sem):
    return pltpu.CompilerParams(dimension_semantics=sem, vmem_limit_bytes=VMEM_LIMIT)


def _sigmoid(x):
    return 1.0 / (1.0 + jnp.exp(-x))


def _softplus(x):
    return jnp.maximum(x, 0.0) + jnp.log(1.0 + jnp.exp(-jnp.abs(x)))


def _gelu(x):
    return 0.5 * x * (1.0 + lax.erf(x * (2.0 ** -0.5)))


def _rms(x, g):
    return x * lax.rsqrt(jnp.mean(x * x, axis=-1, keepdims=True) + EPS) * g


def _dot(a, b, dims=None, precision=None):
    if dims is None:
        dims = (((a.ndim - 1,), (0,)), ((), ()))
    return lax.dot_general(a, b, dims, precision=precision, preferred_element_type=F32)


def _pack_kernel(x_ref, o_ref, *, transpose):
    x = x_ref[...]
    x = x.T if transpose else x
    o_ref[...] = pltpu.bitcast(x.astype(BF16), jnp.uint32)


def _pack_rows(x, transpose):
    n, m = x.shape
    tn = 512
    if transpose:
        out_shape, out_spec = (m // 2, n), pl.BlockSpec((m // 2, tn), lambda i: (0, i))
    else:
        out_shape, out_spec = (n // 2, m), pl.BlockSpec((tn // 2, m), lambda i: (i, 0))
    return pl.pallas_call(
        functools.partial(_pack_kernel, transpose=transpose),
        grid=(n // tn,),
        in_specs=[pl.BlockSpec((tn, m), lambda i: (i, 0))],
        out_specs=out_spec,
        out_shape=jax.ShapeDtypeStruct(out_shape, jnp.uint32),
        compiler_params=_cparams(("parallel",)),
        name="pack_table",
    )(x)


_NT = (((1,), (1,)), ((), ()))
_TN = (((0,), (0,)), ((), ()))


def _adaln_kernel(c_ref, w_ref, b_ref, o_ref):
    c = c_ref[...]
    o_ref[...] = _dot(c * _sigmoid(c), w_ref[...], precision=HI) + b_ref[...]


def _adaln(cond, w_mod, b_mod):
    n, d = cond.shape
    m = w_mod.shape[1]
    tn = 512
    return pl.pallas_call(
        _adaln_kernel,
        grid=(m // tn,),
        in_specs=[pl.BlockSpec((n, d), lambda j: (0, 0)),
                  pl.BlockSpec((d, tn), lambda j: (0, j)),
                  pl.BlockSpec((1, tn), lambda j: (0, j))],
        out_specs=pl.BlockSpec((n, tn), lambda j: (0, j)),
        out_shape=jax.ShapeDtypeStruct((n, m), F32),
        compiler_params=_cparams(("arbitrary",)),
        name="adaln",
    )(cond, w_mod, b_mod.reshape(1, m))


def _premix_kernel(x_ref, mod_ref, g_ref, w_ref, u_ref, qkv_ref, z_ref, ab_ref, *, n_u, n_qkv, n_z):
    m = mod_ref[0]
    h = _rms(x_ref[0], g_ref[...]) * (1.0 + m[1:2]) + m[0:1]
    p = _dot(h.astype(BF16), w_ref[...])
    for j in range(u_ref.shape[0]):
        u_ref[j] = p[:, j * LANES:(j + 1) * LANES].reshape(u_ref.shape[1:])
    qkv_ref[0] = p[:, n_u:n_u + n_qkv]
    z_ref[0] = p[:, n_u + n_qkv:n_u + n_qkv + n_z].astype(BF16)
    ab_ref[0] = p[:, n_u + n_qkv + n_z:]


def _premix(x, mod, per_batch, g, w_pad, n_u, n_qkv, n_z):
    b, l, d = x.shape
    tm = min(512, l)
    tc = S5_CHUNK
    n_all = w_pad.shape[1]
    n_ab = n_all - n_u - n_qkv - n_z
    mod_map = (lambda i, t: (i, 0, 0)) if per_batch else (lambda i, t: (0, 0, 0))
    tok = lambda i, t: (i, t, 0)
    return pl.pallas_call(
        functools.partial(_premix_kernel, n_u=n_u, n_qkv=n_qkv, n_z=n_z),
        grid=(b, l // tm),
        in_specs=[pl.BlockSpec((1, tm, d), tok),
                  pl.BlockSpec((1,) + mod.shape[1:], mod_map),
                  pl.BlockSpec((1, d), lambda i, t: (0, 0)),
                  pl.BlockSpec((d, n_all), lambda i, t: (0, 0))],
        out_specs=[pl.BlockSpec((n_u // LANES, tm // tc, None, tc, LANES), lambda i, t: (0, t, i, 0, 0)),
                   pl.BlockSpec((1, tm, n_qkv), tok),
                   pl.BlockSpec((1, tm, n_z), tok), pl.BlockSpec((1, tm, n_ab), tok)],
        out_shape=[jax.ShapeDtypeStruct((n_u // LANES, l // tc, b, tc, LANES), F32), jax.ShapeDtypeStruct((b, l, n_qkv), F32),
                   jax.ShapeDtypeStruct((b, l, n_z), BF16), jax.ShapeDtypeStruct((b, l, n_ab), F32)],
        compiler_params=_cparams(("parallel", "parallel")),
        name="premix",
    )(x, mod, g, w_pad)


def _s5_matrices(lam_re, lam_im, log_dt, b_re, b_im, c_re, c_im, d_skip):
    t = S5_CHUNK
    n_dirs, g, p = lam_re.shape
    cg = b_re.shape[-1]
    dt = jnp.exp(log_dt)[..., None]
    lr, li = lam_re, lam_im
    den = lr * lr + li * li
    mag = jnp.exp(lr * dt)
    lb_re, lb_im = mag * jnp.cos(li * dt), mag * jnp.sin(li * dt)
    nr = lb_re - 1.0
    cr = (nr * lr + lb_im * li) / den
    ci = (lb_im * lr - nr * li) / den
    bb_re = cr[..., None] * b_re - ci[..., None] * b_im
    bb_im = cr[..., None] * b_im + ci[..., None] * b_re
    taus = jnp.arange(t + 1, dtype=F32)[:, None, None, None]
    pmag = jnp.exp(taus * (lr * dt))
    pw_re = pmag * jnp.cos(taus * (li * dt))
    pw_im = pmag * jnp.sin(taus * (li * dt))
    ein = functools.partial(jnp.einsum, precision=HI)
    lbr = pw_re[..., None] * bb_re - pw_im[..., None] * bb_im
    lbi = pw_re[..., None] * bb_im + pw_im[..., None] * bb_re
    kk = ein('dgcp,tdgpe->tdgce', c_re, lbr) - ein('dgcp,tdgpe->tdgce', c_im, lbi)
    er = c_re[None] * pw_re[:, :, :, None, :] - c_im[None] * pw_im[:, :, :, None, :]
    ei = c_re[None] * pw_im[:, :, :, None, :] + c_im[None] * pw_re[:, :, :, None, :]

    s_idx = jnp.arange(t)
    lag = s_idx[None, :] - s_idx[:, None]
    kf = kk[:, 0][jnp.clip(lag, 0, t)]
    kb = kk[:, 1][jnp.clip(-lag, 0, t)]
    m_full = (jnp.where((lag >= 0)[:, :, None, None, None], kf, 0.0)
              + jnp.where((lag <= 0)[:, :, None, None, None], kb, 0.0))
    eye_t = jnp.eye(t, dtype=F32)[:, :, None, None, None]
    eye_c = jnp.eye(cg, dtype=F32)[None, None, None]
    m_full = m_full + eye_t * eye_c * d_skip.reshape(g, cg)[None, None, :, :, None]
    p_parts = (lbr[:t, 0][::-1], lbi[:t, 0][::-1], lbr[:t, 1], lbi[:t, 1])
    r_parts = (er[1:, 0], -ei[1:, 0], er[1:, 1][::-1], -ei[1:, 1][::-1])

    gw = LANES // cg
    n8 = g // gw
    def spread(width):
        lane = jnp.arange(gw * width)
        tile = (lane[None, :] % width == jnp.arange(width)[:, None]).astype(F32)
        own = (lane[None, :] // width == jnp.arange(gw)[:, None]).astype(F32)
        return tile, own[None, None, :, None, None, :]

    tile_c, own_c = spread(cg)
    tile_p, own_p = spread(p)
    m6 = m_full.reshape(t, t, n8, gw, cg, cg).transpose(2, 0, 3, 5, 1, 4)
    mm = (ein('xsgctd,dl->xsgctl', m6, tile_c) * own_c).reshape(n8, t * LANES, t * LANES)
    p7 = jnp.stack([a.reshape(t, n8, gw, p, cg).transpose(1, 0, 2, 4, 3) for a in p_parts], axis=4)
    pq = (ein('xsgcqn,nl->xsgcql', p7, tile_p) * own_p).reshape(n8, t * LANES, 4 * gw * p)
    r7 = jnp.stack([a.reshape(t, n8, gw, cg, p).transpose(1, 2, 4, 0, 3) for a in r_parts], axis=1)
    rr = (ein('xqgntd,dl->xqgntl', r7, tile_c) * own_c).reshape(n8, 4 * gw * p, t * LANES)
    dec = jnp.stack([x.reshape(n8, gw * p) for x in (pw_re[t, 0], pw_im[t, 0], pw_re[t, 1], pw_im[t, 1])], axis=1)
    dec = jnp.concatenate([dec, jnp.zeros_like(dec)], axis=1)
    return pq.astype(BF16), mm.astype(BF16), rr.astype(BF16), dec


def _chunk_rows(ref, nb, t):
    n = ref.shape[0]
    return jnp.concatenate([ref[:, :, s, :].reshape(n * nb, LANES) for s in range(t)], axis=1).astype(BF16)


def _s5_state_kernel(x_ref, c_ref, pq_ref, dec_ref, sin_ref, loc_ref, locc_ref, *, nb, t, nct, ncx, ncc, sw):
    rt = pl.program_id(1)
    rows = nct * nb
    loc_ref[pl.ds(pl.multiple_of(rt * rows, rows), rows), :] = _dot(_chunk_rows(x_ref, nb, t), pq_ref[0])

    @pl.when(rt == 0)
    def _():
        locc_ref[...] = _dot(_chunk_rows(c_ref, nb, t), pq_ref[0])

    @pl.when(rt == pl.num_programs(1) - 1)
    def _():
        dec = dec_ref[0]
        far, fai, bar, bai = (jnp.broadcast_to(dec[i:i + 1], (nb, sw)) for i in range(4))

        def step(state, ar, ai, ref, k, c0):
            re, im = state
            r = pl.ds(pl.multiple_of(k * nb, nb), nb)
            return ar * re - ai * im + ref[r, c0:c0 + sw], ar * im + ai * re + ref[r, c0 + sw:c0 + 2 * sw]

        zero = jnp.zeros((nb, sw), F32)

        def ctx_body(i, carry):
            f, bk = carry
            return step(f, far, fai, locc_ref, i, 0), step(bk, bar, bai, locc_ref, ncc - 1 - i, 2 * sw)

        carry = lax.fori_loop(0, ncc, ctx_body, ((zero, zero), (zero, zero)))

        def pair_store(k0, c0, s_lo, s_hi):
            r = pl.ds(pl.multiple_of(k0 * nb, 2 * nb), 2 * nb)
            sin_ref[0, r, c0:c0 + sw] = jnp.concatenate([s_lo[0], s_hi[0]], axis=0).astype(BF16)
            sin_ref[0, r, c0 + sw:c0 + 2 * sw] = jnp.concatenate([s_lo[1], s_hi[1]], axis=0).astype(BF16)

        def x_body(j, carry):
            f0, b0 = carry
            f1 = step(f0, far, fai, loc_ref, 2 * j, 0)
            pair_store(2 * j, 0, f0, f1)
            kb = ncx - 2 - 2 * j
            b1 = step(b0, bar, bai, loc_ref, kb + 1, 2 * sw)
            pair_store(kb, 2 * sw, b1, b0)
            return step(f1, far, fai, loc_ref, 2 * j + 1, 0), step(b1, bar, bai, loc_ref, kb, 2 * sw)

        lax.fori_loop(0, ncx // 2, x_body, carry)


def _s5_out_kernel(x_ref, sin_ref, mm_ref, rr_ref, y_ref, *, nb, t):
    n = x_ref.shape[0]
    y = _dot(_chunk_rows(x_ref, nb, t), mm_ref[0]) + _dot(sin_ref[0], rr_ref[0])
    for s in range(t):
        y_ref[:, :, s, :] = y[:, s * LANES:(s + 1) * LANES].reshape(n, nb, LANES)


def _s5(ux, uc, mats):
    pq, mm, rr, dec = mats
    n8, ncx, nb, t, _ = ux.shape
    ncc = uc.shape[1]
    sw = dec.shape[-1]
    assert nb % 8 == 0 and t == S5_CHUNK and ncx % 2 == 0 and n8 == pq.shape[0]
    nct = min(64, ncx)
    ux5, uc5 = ux, uc
    kdim = t * LANES
    xblk = pl.BlockSpec((None, nct, nb, t, LANES), lambda g, r: (g, r, 0, 0, 0))
    wmap = lambda g, r: (g, 0, 0)
    sin = pl.pallas_call(
        functools.partial(_s5_state_kernel, nb=nb, t=t, nct=nct, ncx=ncx, ncc=ncc, sw=sw),
        grid=(n8, ncx // nct),
        in_specs=[xblk, pl.BlockSpec((None, ncc, nb, t, LANES), lambda g, r: (g, 0, 0, 0, 0)),
                  pl.BlockSpec((1, kdim, 4 * sw), wmap), pl.BlockSpec((1, 8, sw), wmap)],
        out_specs=pl.BlockSpec((1, ncx * nb, 4 * sw), wmap),
        out_shape=jax.ShapeDtypeStruct((n8, ncx * nb, 4 * sw), BF16),
        scratch_shapes=[pltpu.VMEM((ncx * nb, 4 * sw), F32), pltpu.VMEM((ncc * nb, 4 * sw), F32)],
        compiler_params=_cparams(("parallel", "arbitrary")),
        name="s5_state",
    )(ux5, uc5, pq, dec)
    y = pl.pallas_call(
        functools.partial(_s5_out_kernel, nb=nb, t=t),
        grid=(n8, ncx // nct),
        in_specs=[xblk, pl.BlockSpec((1, nct * nb, 4 * sw), lambda g, r: (g, r, 0)),
                  pl.BlockSpec((1, kdim, kdim), wmap), pl.BlockSpec((1, 4 * sw, kdim), wmap)],
        out_specs=xblk,
        out_shape=jax.ShapeDtypeStruct((n8, ncx, nb, t, LANES), F32),
        compiler_params=_cparams(("parallel", "parallel")),
        name="s5_out",
    )(ux5, sin, mm, rr)
    return y


_CONV_PAD = 72


def _gdn_conv_kernel(x_ref, w_ref, o_ref, pad_ref, *, length, cols, two_d, n_norm, n_q, q_scale):
    cb = pl.program_id(1)
    pad = _CONV_PAD
    zeros = jnp.zeros((pad, LANES), F32)
    pad_ref[0:pad, :] = zeros
    pad_ref[pad + length:pad + length + pad, :] = zeros
    pad_ref[pad:pad + length, :] = x_ref[0]
    rc = min(256, length)
    for r0 in range(0, length, rc):
        col = lax.rem(lax.broadcasted_iota(jnp.int32, (rc, LANES), 0) + r0, cols)
        acc = jnp.zeros((rc, LANES), F32)
        for dr in ((0, 1, 2) if two_d else (1,)):
            for dc in range(3):
                off = (dr - 1) * cols + (dc - 1)
                term = pad_ref[pad + r0 + off:pad + r0 + off + rc, :] * w_ref[dr * 3 + dc:dr * 3 + dc + 1, :]
                if dc == 0:
                    term = jnp.where(col >= 1, term, 0.0)
                if dc == 2:
                    term = jnp.where(col <= cols - 2, term, 0.0)
                acc = acc + term
        y = acc * _sigmoid(acc)
        nrm = y * lax.rsqrt(jnp.sum(y * y, axis=-1, keepdims=True) + EPS)
        nrm = nrm * jnp.where(cb < n_q, q_scale, 1.0)
        o_ref[0, r0:r0 + rc, :] = jnp.where(cb < n_norm, nrm, y)


def _gdn_conv(qkv, conv_w, cols, two_d, n_heads, head_dim):
    b, l, ch = qkv.shape
    w9 = jnp.concatenate([conv_w.reshape(9, ch), jnp.zeros((7, ch), F32)], axis=0)
    blk = lambda i, j: (i, 0, j)
    return pl.pallas_call(
        functools.partial(_gdn_conv_kernel, length=l, cols=cols, two_d=two_d,
                          n_norm=2 * n_heads, n_q=n_heads, q_scale=head_dim ** -0.5),
        grid=(b, ch // LANES),
        in_specs=[pl.BlockSpec((1, l, LANES), blk), pl.BlockSpec((16, LANES), lambda i, j: (0, j))],
        out_specs=pl.BlockSpec((1, l, LANES), blk),
        out_shape=jax.ShapeDtypeStruct((b, l, ch), F32),
        scratch_shapes=[pltpu.VMEM((l + 2 * _CONV_PAD, LANES), F32)],
        compiler_params=_cparams(("parallel", "parallel")),
        name="gdn_conv",
    )(qkv, w9)


def _gdn_kernel(qx_ref, kx_ref, vx_ref, abx_ref, qc_ref, kc_ref, vc_ref, abc_ref,
                pl_ref, o_ref, gcx_ref, bcx_ref, gcc_ref, bcc_ref,
                u_ref, w_ref, qd_ref, kd_ref, at_ref, gt_ref, *, n_heads, ncx, ncc):
    cs = GDN_CHUNK
    hd = qx_ref.shape[-1]
    head = pl.program_id(1)
    alog_l, dtb_l = pl_ref[0:1, :], pl_ref[1:2, :]
    lane = lax.broadcasted_iota(jnp.int32, (1, LANES), 1)

    def gate_columns(ab_ref, g_out, b_out, n_rows):
        rc = min(256, n_rows)
        for r0 in range(0, n_rows, rc):
            ab = ab_ref[0, r0:r0 + rc, :]
            g_all = -jnp.exp(alog_l) * _softplus(ab + dtb_l)
            b_all = _sigmoid(ab)
            for dr in range(2):
                j = dr * n_heads + head
                g = jnp.sum(jnp.where(lane == j, g_all, 0.0), axis=-1, keepdims=True)
                bt = jnp.sum(jnp.where(lane == j + 2 * n_heads, b_all, 0.0), axis=-1, keepdims=True)
                g_out[dr, r0:r0 + rc, :] = jnp.broadcast_to(g, (rc, hd))
                b_out[dr, r0:r0 + rc, :] = jnp.broadcast_to(bt, (rc, hd))

    gate_columns(abx_ref, gcx_ref, bcx_ref, ncx * cs)
    gate_columns(abc_ref, gcc_ref, bcc_ref, ncc * cs)

    nblk = 4
    side = nblk * cs
    def block_masks():
        ri = lax.broadcasted_iota(jnp.int32, (side, side), 0)
        ci = lax.broadcasted_iota(jnp.int32, (side, side), 1)
        dist = jnp.where(ri < side // 2, ri - ci, ci - ri)
        dist = jnp.where((ri // cs) == (ci // cs), dist, -1)
        return dist >= 0, dist > 0, jnp.where(ri == ci, 1.0, 0.0)

    def cumsum_blocks(x):
        row_id = lax.broadcasted_iota(jnp.int32, (side, hd), 0)
        pos = row_id % cs
        is_fwd = row_id < side // 2
        s = 1
        while s < cs:
            x = x + jnp.where(is_fwd, jnp.where(pos >= s, pltpu.roll(x, s, 0), 0.0),
                              jnp.where(pos < cs - s, pltpu.roll(x, side - s, 0), 0.0))
            s *= 2
        return x

    def each(f, *lists):
        return [f(*args) for args in zip(*lists)]

    def chunks_local(c0s, refs, base):
        q_ref, k_ref, v_ref, gcol_ref, bcol_ref = refs
        incl, strict, eye = block_masks()
        rows = [pl.ds(pl.multiple_of(c0 * cs, 2 * cs), 2 * cs) for c0 in c0s]
        q, k, v = ([jnp.concatenate([r[0, rw, :]] * 2, axis=0) for rw in rows] for r in (q_ref, k_ref, v_ref))
        gb, bb = ([jnp.concatenate([r[0, rw, :], r[1, rw, :]], axis=0) for rw in rows] for r in (gcol_ref, bcol_ref))
        gcs = each(cumsum_blocks, gb)
        gwide = each(lambda g: jnp.concatenate([g] * (side // hd), axis=1), gcs)
        decay = each(lambda g: jnp.where(incl, jnp.exp(jnp.where(incl, g - g.T[0:1, :], 0.0)), 0.0), gwide)
        kbf = each(lambda t: t.astype(BF16), k)
        kb = each(lambda t, b_: t * b_, k, bb)
        lmat = each(lambda x, y, dc: jnp.where(strict, _dot(x.astype(BF16), y, _NT) * dc, 0.0), kb, kbf, decay)
        pw = each(lambda l_: -l_, lmat)
        tmat = each(lambda p_: eye + p_, pw)
        for _ in range(cs.bit_length() - 2):
            pw = each(lambda p_: _dot(p_.astype(BF16), p_.astype(BF16)), pw)
            tmat = each(lambda t_, p_: t_ + _dot(t_.astype(BF16), p_.astype(BF16)), tmat, pw)
        g_exp = each(jnp.exp, gcs)
        rhs = each(lambda v_, b_, kb_, ge: jnp.concatenate([v_ * b_, kb_ * ge], axis=1).astype(BF16), v, bb, kb, g_exp)
        uw = each(lambda t_, r_: _dot(t_.astype(BF16), r_), tmat, rhs)
        attn = each(lambda q_, y, dc: jnp.where(incl, _dot(q_.astype(BF16), y, _NT) * dc, 0.0), q, kbf, decay)
        half = side // 2
        for c0, uw_, attn_, q_, k_, g_, ge in zip(c0s, uw, attn, q, k, gcs, g_exp):
            lasts = [g_[(j + 1) * cs - 1:(j + 1) * cs, :] if j < nblk // 2 else g_[j * cs:j * cs + 1, :] for j in range(nblk)]
            g_last = jnp.concatenate([jnp.broadcast_to(r, (cs, hd)) for r in lasts], axis=0)
            qd = (q_ * ge).astype(BF16)
            kd = (k_ * jnp.exp(g_last - g_)).astype(BF16)
            srows = pl.ds(pl.multiple_of((base + c0) * cs, 2 * cs), 2 * cs)
            grows = pl.ds(pl.multiple_of((base + c0) * 8, 16), 16)
            for dr in range(2):
                blk = slice(dr * half, (dr + 1) * half)
                u_ref[dr, srows, :] = uw_[blk, :hd]
                w_ref[dr, srows, :] = uw_[blk, hd:].astype(BF16)
                qd_ref[dr, srows, :] = qd[blk]
                kd_ref[dr, srows, :] = kd[blk]
                at_ref[dr, srows, :] = jnp.concatenate(
                    [attn_[j * cs:(j + 1) * cs, j * cs:(j + 1) * cs] for j in (2 * dr, 2 * dr + 1)], axis=0).astype(BF16)
                gt_ref[dr, grows, :] = jnp.concatenate(
                    [jnp.broadcast_to(jnp.exp(lasts[j]), (8, hd)) for j in (2 * dr, 2 * dr + 1)], axis=0)

    def chunks_seq(cgs, states):
        drs = (0, 1)
        srows = [pl.ds(pl.multiple_of(cg * cs, cs), cs) for cg in cgs]
        sb = [st.astype(BF16) for st in states]
        vb = [(u_ref[dr, r, :] - _dot(w_ref[dr, r, :], s_)).astype(BF16) for dr, r, s_ in zip(drs, srows, sb)]
        new = [st * gt_ref[dr, pl.ds(pl.multiple_of(cg * 8, 8), 8), :][0:1] + _dot(kd_ref[dr, r, :], v_, _TN)
               for dr, cg, r, st, v_ in zip(drs, cgs, srows, states, vb)]
        outs = [_dot(qd_ref[dr, r, :], s_) + _dot(at_ref[dr, r, :], v_) for dr, r, s_, v_ in zip(drs, srows, sb, vb)]
        return outs, new

    refs_c = (qc_ref, kc_ref, vc_ref, gcc_ref, bcc_ref)
    refs_x = (qx_ref, kx_ref, vx_ref, gcx_ref, bcx_ref)

    def local_pass(n, refs, base):
        per = 8 if n % 8 == 0 else 4

        def body(i, _):
            chunks_local([per * i + 2 * j for j in range(per // 2)], refs, base)
            return 0

        lax.fori_loop(0, n // per, body, 0)

    local_pass(ncc, refs_c, 0)
    local_pass(ncx, refs_x, ncc)

    zero = jnp.zeros((hd, hd), F32)

    def ctx_body(i, carry):
        return tuple(chunks_seq((i, ncc - 1 - i), carry)[1])

    carry = lax.fori_loop(0, ncc, ctx_body, (zero, zero))
    o_ref[...] = jnp.zeros_like(o_ref)

    def x_body(i, carry):
        ib = ncx - 1 - i
        (of, ob), new = chunks_seq((ncc + i, ncc + ib), carry)
        o_ref[0, pl.ds(pl.multiple_of(i * cs, cs), cs), :] += of
        o_ref[0, pl.ds(pl.multiple_of(ib * cs, cs), cs), :] += ob
        return tuple(new)

    lax.fori_loop(0, ncx, x_body, carry)


def _gdn(qkvx, abx, qkvc, abc, a_log, dt_bias, n_heads, head_dim):
    b, l, _ = qkvx.shape
    lc = qkvc.shape[1]
    cs = GDN_CHUNK
    ncx, ncc = l // cs, lc // cs
    assert ncx % 4 == 0 and ncc % 4 == 0, "chunks are processed four at a time"
    ng = 2 * n_heads
    p_l = jnp.zeros((8, LANES), F32)
    p_l = p_l.at[0, :ng].set(a_log.reshape(ng)).at[1, :ng].set(dt_bias.reshape(ng))

    def col(off):
        return lambda i, h: (i, 0, off + h)

    full = lambda i, h: (i, 0, 0)
    const = lambda i, h: (0, 0)
    hb = head_dim
    return pl.pallas_call(
        functools.partial(_gdn_kernel, n_heads=n_heads, ncx=ncx, ncc=ncc),
        grid=(b, n_heads),
        in_specs=[pl.BlockSpec((1, l, hb), col(0)), pl.BlockSpec((1, l, hb), col(n_heads)),
                  pl.BlockSpec((1, l, hb), col(2 * n_heads)),
                  pl.BlockSpec((1, l, LANES), full),
                  pl.BlockSpec((1, lc, hb), col(0)), pl.BlockSpec((1, lc, hb), col(n_heads)),
                  pl.BlockSpec((1, lc, hb), col(2 * n_heads)),
                  pl.BlockSpec((1, lc, LANES), full),
                  pl.BlockSpec((8, LANES), const)],
        out_specs=pl.BlockSpec((1, l, hb), col(0)),
        out_shape=jax.ShapeDtypeStruct((b, l, n_heads * head_dim), F32),
        scratch_shapes=[pltpu.VMEM((2, l, hb), F32), pltpu.VMEM((2, l, hb), F32),
                        pltpu.VMEM((2, lc, hb), F32), pltpu.VMEM((2, lc, hb), F32),
                        pltpu.VMEM((2, l + lc, hb), F32), pltpu.VMEM((2, l + lc, hb), BF16),
                        pltpu.VMEM((2, l + lc, hb), BF16), pltpu.VMEM((2, l + lc, hb), BF16),
                        pltpu.VMEM((2, l + lc, cs), BF16), pltpu.VMEM((2, (ncx + ncc) * 8, hb), F32)],
        compiler_params=_cparams(("parallel", "arbitrary")),
        name="gdn",
    )(qkvx, qkvx, qkvx, abx, qkvc, qkvc, qkvc, abc, p_l)


def _postmix_kernel(x_ref, y_ref, o_ref, z_ref, mod_ref, gn_ref, gpost_ref, wglu_ref, wout_ref, out_ref,
                    *, n_heads, head_dim, s5_w):
    s = _gelu(jnp.concatenate([y_ref[j].reshape(x_ref.shape[1], LANES) for j in range(y_ref.shape[0])], axis=1))
    s = s * _sigmoid(_dot(s.astype(BF16), wglu_ref[...]))
    o = o_ref[0]
    z = z_ref[0].astype(F32)
    parts = []
    for h in range(n_heads):
        sl = slice(h * head_dim, (h + 1) * head_dim)
        zh = z[:, sl]
        parts.append(_rms(o[:, sl], gn_ref[...]) * (zh * _sigmoid(zh)))
    gd = jnp.concatenate(parts, axis=-1)
    mixed = _dot(s.astype(BF16), wout_ref[0:s5_w, :]) + _dot(gd.astype(BF16), wout_ref[s5_w:, :])
    gate = mod_ref[0][2:3]
    out_ref[0] = x_ref[0] + gate * _rms(mixed, gpost_ref[...])


def _postmix(x, y, o, z, mod, gdn_norm, g_post, w_glu, w_out, n_heads, head_dim):
    b, l, d = x.shape
    s5_w = y.shape[0] * y.shape[-1]
    gw = o.shape[-1]
    tm = min(512, l)
    tc = S5_CHUNK
    tok = lambda i, t: (i, t, 0)
    const = lambda i, t: (0, 0)
    return pl.pallas_call(
        functools.partial(_postmix_kernel, n_heads=n_heads, head_dim=head_dim, s5_w=s5_w),
        grid=(b, l // tm),
        in_specs=[pl.BlockSpec((1, tm, d), tok), pl.BlockSpec((s5_w // LANES, tm // tc, None, tc, LANES), lambda i, t: (0, t, i, 0, 0)),
                  pl.BlockSpec((1, tm, gw), tok), pl.BlockSpec((1, tm, gw), tok),
                  pl.BlockSpec((1,) + mod.shape[1:], lambda i, t: (i, 0, 0)),
                  pl.BlockSpec((1, head_dim), const), pl.BlockSpec((1, d), const),
                  pl.BlockSpec((s5_w, s5_w), const), pl.BlockSpec((s5_w + gw, d), const)],
        out_specs=pl.BlockSpec((1, tm, d), tok),
        out_shape=jax.ShapeDtypeStruct((b, l, d), F32),
        compiler_params=_cparams(("parallel", "parallel")),
        name="postmix",
    )(x, y, o, z, mod, gdn_norm, g_post, w_glu, w_out)


def _peer_prep_kernel(x_ref, mod_ref, g_ref, wq_ref, keys_ref, ht_ref, r2_ref, e2_ref, n1_ref, e1_ref,
                      s1_ref, s2_ref, lv1_ref, lv2_ref, tau_ref, *, n_keys, topk, tt):
    @pl.when(pl.program_id(1) == 0)
    def _():
        m = mod_ref[0]
        h = _rms(x_ref[0], g_ref[...]) * (1.0 + m[4:5]) + m[3:4]
        ht_ref[...] = h.T.astype(BF16)

    half = keys_ref.shape[-1]
    ht = ht_ref[...]
    q1 = _dot(wq_ref[0:half, :], ht).astype(BF16)
    q2 = _dot(wq_ref[half:2 * half, :], ht).astype(BF16)
    s1_ref[...] = _dot(keys_ref[0, 0], q1)
    s2_ref[...] = _dot(keys_ref[0, 1], q2)
    kf = float(topk)

    nt = tt // LANES

    def lane_tile(t):
        return pl.ds(pl.multiple_of(t * LANES, LANES), LANES)

    def levels(t, _):
        cols = lane_tile(t)

        def lv_body(q, carry):
            c1, c2, rank2 = carry
            m1 = jnp.max(c1, axis=0, keepdims=True)
            m2 = jnp.max(c2, axis=0, keepdims=True)
            lv1_ref[t, pl.ds(q, 1), :] = m1
            lv2_ref[t, pl.ds(q, 1), :] = m2
            h1 = c1 == jnp.where(m1 > NEG, m1, -NEG)
            h2 = c2 == jnp.where(m2 > NEG, m2, -NEG)
            rank2 = jnp.where(h2, jnp.asarray(q, F32), rank2)
            return jnp.where(h1, NEG, c1), jnp.where(h2, NEG, c2), rank2

        s1, s2 = s1_ref[:, cols], s2_ref[:, cols]
        _, _, rank2 = lax.fori_loop(0, topk, lv_body, (s1, s2, jnp.full(s2.shape, kf, F32)))
        r2_ref[:, cols] = rank2.astype(BF16)
        return 0

    lax.fori_loop(0, nt, levels, 0)

    a_all, b_all = (jnp.concatenate([r[j] for j in range(nt)], axis=1) for r in (lv1_ref, lv2_ref))
    sub = topk // 2
    cand = jnp.concatenate([a_all[0:1] + b_all] + [a_all[p:p + 1] + b_all[0:sub] for p in range(1, sub)]
                           + [a_all[sub:] + b_all[0:1]], axis=0)

    def tau_body(_, carry):
        cand, cnt, tau = carry
        mx = jnp.max(cand, axis=0, keepdims=True)
        hit = cand == mx
        tau = jnp.where(cnt < kf, mx, tau)
        cnt = cnt + jnp.sum(jnp.where(hit, 1.0, 0.0), axis=0, keepdims=True)
        return jnp.where(hit, NEG, cand), cnt, tau

    row0 = a_all[0:1]
    _, _, tau_all = lax.fori_loop(0, topk, tau_body, (cand, jnp.zeros_like(row0), jnp.full_like(row0, NEG)))
    for j in range(nt):
        tau_ref[j] = jnp.broadcast_to(tau_all[:, j * LANES:(j + 1) * LANES], tau_ref.shape[1:])

    def gates(t, _):
        cols = lane_tile(t)
        s1, s2 = s1_ref[:, cols], s2_ref[:, cols]
        a_mat, b_mat, tau = lv1_ref[t], lv2_ref[t], tau_ref[t][0:1]
        a0, b0 = a_mat[0:1], b_mat[0:1]

        def n_body(q, carry):
            n1, c2 = carry
            b = lv2_ref[t, pl.ds(q, 1), :]
            n1 = n1 + jnp.where(s1 + b >= tau, 1.0, 0.0)
            c2 = c2 + jnp.where(a_mat + b >= tau, jnp.exp(b - b0), 0.0)
            return n1, c2

        n1, c2 = lax.fori_loop(0, topk, n_body, (jnp.zeros_like(s1), jnp.zeros_like(a_mat)))
        z = jnp.sum(jnp.exp(a_mat - a0) * c2, axis=0, keepdims=True)
        e2_ref[:, cols] = (jnp.exp(s2 - b0) / z).astype(BF16)
        n1_ref[0, :, cols] = n1
        e1_ref[0, :, cols] = jnp.exp(s1 - a0)
        return 0

    lax.fori_loop(0, nt, gates, 0)


def _peer_main_kernel(ht_ref, u_ref, vt_ref, r2_ref, e2_ref, n1_ref, e1_ref, x_ref, mod_ref, g_ref,
                      out_ref, acc_ref, hs0_ref, hs1_ref, act0_ref, act1_ref, r2s_ref, e2s_ref, fence_ref,
                      *, n_heads, n_keys, eb, tt):
    k = pl.program_id(1)
    nsub = eb // n_keys
    pk = BF16_ROWS
    grp = n_keys // pk

    last = pl.num_programs(1) - 1

    @pl.when(k == 0)
    def _():
        acc_ref[...] = jnp.zeros_like(acc_ref)
        r2s_ref[...] = r2_ref[...]
        e2s_ref[...] = e2_ref[...]

    def gate_tiles(hs_ref, act_ref, blk, ii):
        i = blk * nsub + ii
        nrows, erows = n1_ref[i], e1_ref[i]
        rs = slice(ii * n_keys, (ii + 1) * n_keys)
        for tc in range(tt // LANES):
            cs_ = slice(tc * LANES, (tc + 1) * LANES)
            gsum = jnp.zeros((n_keys, LANES), BF16)
            for hd in range(n_heads):
                nb, ebc = (jnp.concatenate([jnp.broadcast_to(r[hd:hd + 1, cs_], (pk, LANES)).astype(BF16)] * grp, 0)
                           for r in (nrows, erows))
                ks = slice(hd * n_keys, (hd + 1) * n_keys)
                r2, e2 = r2s_ref[ks, cs_], e2s_ref[ks, cs_]
                gsum = gsum + jnp.where(r2 < nb, e2, jnp.zeros_like(e2)) * ebc
            act_ref[rs, cs_] = _gelu(hs_ref[rs, cs_]).astype(BF16) * gsum

    d = ht_ref.shape[0]
    npre, nmix = 4, 2
    kc = d // npre
    ec = eb // nmix

    def half_step(u_rows, hs_w, hs_r, act_w, act_r, blk, pre=True, gate=True, mix=True):
        for ii in range(nsub):
            if pre and ii % (nsub // npre) == 0:
                p = ii // (nsub // npre)
                u_part = pltpu.bitcast(u_ref[u_rows // 2:(u_rows + eb) // 2, p * kc:(p + 1) * kc], BF16)
                part = _dot(u_part, ht_ref[p * kc:(p + 1) * kc, :])
                if p == 0:
                    hs_w[...] = part
                else:
                    hs_w[...] += part
            if gate:
                gate_tiles(hs_r, act_w, blk, ii)
            if mix and ii % (nsub // nmix) == 0:
                j = ii // (nsub // nmix)
                vt_part = pltpu.bitcast(vt_ref[:, u_rows + j * ec:u_rows + (j + 1) * ec], BF16)
                acc_ref[...] += _dot(vt_part, act_r[j * ec:(j + 1) * ec, :])

    def step(first, final):
        half_step(0, hs0_ref, hs1_ref, act1_ref, act0_ref, 2 * k - 1, pre=not final, gate=not first, mix=not first)
        pl.semaphore_signal(fence_ref, 1)
        pl.semaphore_wait(fence_ref, 1)
        half_step(eb, hs1_ref, hs0_ref, act0_ref, act1_ref, 2 * k, pre=not final, gate=not final, mix=not first)

    @pl.when(k == 0)
    def _():
        step(True, False)

    @pl.when((k > 0) & (k < last))
    def _():
        step(False, False)

    @pl.when(k == last)
    def _():
        step(False, True)
        o = acc_ref[...].T
        out_ref[0] = x_ref[0] + mod_ref[0][5:6] * _rms(o, g_ref[...])


def _peer(x1, mod, g_pre, g_post, w_q, keys, exp_u, exp_v):
    b, l, d = x1.shape
    n_heads, _, n_keys, half = keys.shape
    n_exp = exp_u.shape[0]
    ntok = b * l
    tt = min(512, l)
    eb = 512
    tpb = l // tt
    wq_t = w_q.T.astype(BF16)
    keys_b = keys.astype(BF16)
    u_b = _pack_rows(exp_u, False)
    vt_b = _pack_rows(exp_v, True)
    tokb = lambda t, h: (t // tpb, t % tpb, 0)
    modb = lambda t, h: (t // tpb, 0, 0)
    hmap = lambda t, h: (h, 0, t)
    hshape = jax.ShapeDtypeStruct((n_heads, n_keys, ntok), F32)
    hshape_b = jax.ShapeDtypeStruct((n_heads * n_keys, ntok), BF16)
    ht, r2, e2, n1, e1 = pl.pallas_call(
        functools.partial(_peer_prep_kernel, n_keys=n_keys, topk=PEER_TOPK, tt=tt),
        grid=(ntok // tt, n_heads),
        in_specs=[pl.BlockSpec((1, tt, d), tokb), pl.BlockSpec((1,) + mod.shape[1:], modb),
                  pl.BlockSpec((1, d), lambda t, h: (0, 0)),
                  pl.BlockSpec((2 * half, d), lambda t, h: (h, 0)),
                  pl.BlockSpec((1, 2, n_keys, half), lambda t, h: (h, 0, 0, 0))],
        out_specs=[pl.BlockSpec((d, tt), lambda t, h: (0, t))] + [pl.BlockSpec((n_keys, tt), lambda t, h: (h, t))] * 2
                  + [pl.BlockSpec((1, n_keys, tt), hmap)] * 2,
        out_shape=[jax.ShapeDtypeStruct((d, ntok), BF16), hshape_b, hshape_b, hshape, hshape],
        scratch_shapes=[pltpu.VMEM((n_keys, tt), F32), pltpu.VMEM((n_keys, tt), F32),
                        pltpu.VMEM((tt // LANES, PEER_TOPK, LANES), F32), pltpu.VMEM((tt // LANES, PEER_TOPK, LANES), F32),
                        pltpu.VMEM((tt // LANES, 8, LANES), F32)],
        compiler_params=_cparams(("parallel", "arbitrary")),
        name="peer_prep",
    )(x1, mod, g_pre, wq_t, keys_b)

    n_pairs = n_exp // (2 * eb)
    tokb2 = lambda t, k: (t // tpb, t % tpb, 0)
    hmap2 = lambda t, k: (0, 0, t)
    return pl.pallas_call(
        functools.partial(_peer_main_kernel, n_heads=n_heads, n_keys=n_keys, eb=eb, tt=tt),
        grid=(ntok // tt, n_pairs + 1),
        in_specs=[pl.BlockSpec((d, tt), lambda t, k: (0, t)),
                  pl.BlockSpec((eb, d), lambda t, k: (jnp.minimum(k, n_pairs - 1), 0)),
                  pl.BlockSpec((d // 2, 2 * eb), lambda t, k: (0, jnp.maximum(k - 1, 0)))]
                 + [pl.BlockSpec((n_heads * n_keys, tt), lambda t, k: (0, t))] * 2
                 + [pl.BlockSpec((n_keys, n_heads, tt), hmap2)] * 2
                 + [pl.BlockSpec((1, tt, d), tokb2),
                    pl.BlockSpec((1,) + mod.shape[1:], lambda t, k: (t // tpb, 0, 0)),
                    pl.BlockSpec((1, d), lambda t, k: (0, 0))],
        out_specs=pl.BlockSpec((1, tt, d), tokb2),
        out_shape=jax.ShapeDtypeStruct((b, l, d), F32),
        scratch_shapes=[pltpu.VMEM((d, tt), F32), pltpu.VMEM((eb, tt), F32), pltpu.VMEM((eb, tt), F32),
                        pltpu.VMEM((eb, tt), BF16), pltpu.VMEM((eb, tt), BF16),
                        pltpu.VMEM((n_heads * n_keys, tt), BF16), pltpu.VMEM((n_heads * n_keys, tt), BF16),
                        pltpu.SemaphoreType.REGULAR],
        compiler_params=_cparams(("parallel", "arbitrary")),
        name="peer_main",
    )(ht, u_b, vt_b, r2, e2, n1.transpose(1, 0, 2), e1.transpose(1, 0, 2), x1, mod, g_post)


def kernel(x, c, ctx, c_ctx, w_mod, b_mod, g_pre_mix, g_post_mix, g_pre_ffn, g_post_ffn, w_in, w_out,
           s5_lam_re, s5_lam_im, s5_log_dt, s5_b_re, s5_b_im, s5_c_re, s5_c_im, s5_d, s5_w_glu,
           gdn_conv, gdn_a_log, gdn_dt_bias, gdn_norm, peer_w_q, peer_keys, peer_u, peer_v):
    assert w_mod.shape[0] == 1, "single layer: the context stream only feeds the latent scans"
    b, l, d = x.shape
    lc = ctx.shape[1]
    n_heads = gdn_a_log.shape[-1]
    head_dim = gdn_norm.shape[-1]
    gdn_w = n_heads * head_dim
    s5_w = s5_d.shape[-1]
    n_gate = 2 * 2 * n_heads

    n_cond = -(-(b + 1) // 8) * 8
    cond = jnp.concatenate([c, c_ctx[None], jnp.zeros((n_cond - b - 1, d), F32)], axis=0)
    mod = _adaln(cond, w_mod[0], b_mod[0]).reshape(n_cond, 6, d)

    n_proj = w_in.shape[-1]
    w_pad = jnp.pad(w_in[0], ((0, 0), (0, -n_proj % LANES))).astype(BF16)
    g1 = g_pre_mix[0][None]
    ux, qkvx, zx, abx = _premix(x, mod[:b], True, g1, w_pad, s5_w, 3 * gdn_w, gdn_w)
    uc, qkvc, _, abc = _premix(ctx, mod[b:b + 1], False, g1, w_pad, s5_w, 3 * gdn_w, gdn_w)
    assert abx.shape[-1] == LANES and n_gate <= LANES

    mats = _s5_matrices(s5_lam_re[0], s5_lam_im[0], s5_log_dt[0], s5_b_re[0], s5_b_im[0],
                        s5_c_re[0], s5_c_im[0], s5_d[0])
    y_s5 = _s5(ux, uc, mats)

    qkvx = _gdn_conv(qkvx, gdn_conv[0], GRID_W, l // GRID_W > 1, n_heads, head_dim)
    qkvc = _gdn_conv(qkvc, gdn_conv[0], lc, False, n_heads, head_dim)
    o_gdn = _gdn(qkvx, abx, qkvc, abc, gdn_a_log[0], gdn_dt_bias[0], n_heads, head_dim)

    x1 = _postmix(x, y_s5, o_gdn, zx, mod[:b], gdn_norm[0][None], g_post_mix[0][None],
                  s5_w_glu[0].astype(BF16), w_out[0].astype(BF16), n_heads, head_dim)
    return _peer(x1, mod[:b], g_pre_ffn[0][None], g_post_ffn[0][None],
                 peer_w_q[0], peer_keys[0], peer_u[0], peer_v[0])
```
